```python
import jax, jax.numpy as jnp
from jax import lax
import numpy as np

D_MODEL = 1024
BATCH = 2
SEQ = 8192
DEPTH = 1

N_MEM = 256
MLA_HEADS = 8
MLA_NOPE = 64
MLA_ROPE = 32
MLA_V = 64
MLA_Q_RANK = 256
MLA_KV_RANK = 128
MLA_WIDTH = MLA_HEADS * MLA_V
RET_HEADS = 4
RET_DK = 128
RET_DV = 128
RET_CHUNK = 128
RET_WIDTH = RET_HEADS * RET_DV
X_HEADS = 4
X_HEAD_DIM = D_MODEL // X_HEADS
D_FF = 2816
CONV_W = 3
Q_BLOCK = 128
ROPE_THETA = 10000.0
EPS = 1e-6
IN_WIDTH = MLA_Q_RANK + MLA_KV_RANK + MLA_ROPE + 2 * RET_HEADS * RET_DK + 2 * RET_WIDTH + 2 * D_MODEL

kernel_name = "hybrid_mla_retention_gated_block"


def rms_norm(x, g):
    xf = x.astype(jnp.float32)
    y = xf * lax.rsqrt(jnp.mean(xf * xf, axis=-1, keepdims=True) + EPS)
    return (y * g.astype(jnp.float32)).astype(x.dtype)


def rotary(x, positions):
    d = x.shape[-1]
    half = d // 2
    inv = ROPE_THETA ** (-jnp.arange(half, dtype=jnp.float32) / half)
    ang = positions.astype(jnp.float32)[:, :, None] * inv
    cos = jnp.cos(ang)[:, :, None, :]
    sin = jnp.sin(ang)[:, :, None, :]
    x1 = x[..., :half].astype(jnp.float32)
    x2 = x[..., half:].astype(jnp.float32)
    return jnp.concatenate([x1 * cos - x2 * sin, x2 * cos + x1 * sin], axis=-1).astype(x.dtype)


def causal_block_attention(q, k, v, scale):
    B, H, S, dq = q.shape
    dv = v.shape[-1]
    nb = S // Q_BLOCK
    qb = q.reshape(B, H, nb, Q_BLOCK, dq).transpose(2, 0, 1, 3, 4)
    kpos = jnp.arange(S)
    neg = jnp.finfo(jnp.float32).min

    def one(args):
        q_blk, i = args
        s = jnp.einsum('bhqd,bhkd->bhqk', q_blk, k).astype(jnp.float32) * scale
        qpos = i * Q_BLOCK + jnp.arange(Q_BLOCK)
        s = jnp.where(kpos[None, :] <= qpos[:, None], s, neg)
        p = jax.nn.softmax(s, axis=-1).astype(v.dtype)
        return jnp.einsum('bhqk,bhkd->bhqd', p, v)

    o = lax.map(one, (qb, jnp.arange(nb)))
    return o.transpose(1, 2, 0, 3, 4).reshape(B, H, S, dv)


def retention_chunkwise(q, k, v):
    B, S, H, dk = q.shape
    dv = v.shape[-1]
    C = RET_CHUNK
    nc = S // C
    log_g = jnp.log1p(-jnp.exp2(-5.0 - jnp.arange(H, dtype=jnp.float32)))
    idx = jnp.arange(C, dtype=jnp.float32)
    rel = idx[:, None] - idx[None, :]
    dmask = jnp.where(rel >= 0, jnp.exp(log_g[:, None, None] * jnp.maximum(rel, 0.0)), 0.0)
    zeta = jnp.exp(log_g[None, :] * (C - 1.0 - idx)[:, None])
    xi = jnp.exp(log_g[None, :] * (idx + 1.0)[:, None])
    chunk_decay = jnp.exp(log_g * C)

    qc = q.reshape(B, nc, C, H, dk)
    kc = k.reshape(B, nc, C, H, dk)
    vc = v.reshape(B, nc, C, H, dv)
    s = jnp.einsum('bcnhd,bcmhd->bchnm', qc, kc) * dmask.astype(q.dtype)[None, None]
    inner = jnp.einsum('bchnm,bcmhe->bcnhe', s, vc)
    kv = jnp.einsum('bcmhd,bcmhe->cbhde', kc * zeta.astype(q.dtype)[None, None, :, :, None], vc).astype(jnp.float32)

    def step(R, kv_c):
        return R * chunk_decay[None, :, None, None] + kv_c, R

    _, R_prev = lax.scan(step, jnp.zeros((B, H, dk, dv), jnp.float32), kv)
    cross = jnp.einsum('bcnhd,cbhde->bcnhe', qc * xi.astype(q.dtype)[None, None, :, :, None], R_prev.astype(q.dtype))
    return (inner + cross).reshape(B, S, H, dv)


def head_group_norm(y, g):
    B, S, H, d = y.shape
    yf = y.astype(jnp.float32)
    mu = jnp.mean(yf, axis=-1, keepdims=True)
    var = jnp.mean(jnp.square(yf - mu), axis=-1, keepdims=True)
    yn = ((yf - mu) * lax.rsqrt(var + EPS)).reshape(B, S, H * d)
    return (yn * g.astype(jnp.float32)).astype(y.dtype)


def causal_dwconv(u, w, b):
    C = u.shape[-1]
    y = lax.conv_general_dilated(u, w[:, None, :].astype(u.dtype), window_strides=(1,),
                                 padding=[(CONV_W - 1, 0)],
                                 dimension_numbers=('NWC', 'WIO', 'NWC'),
                                 feature_group_count=C)
    return y + b


def setup_inputs(seed: int = 0) -> dict:
    key = jax.random.key(seed)
    ks = jax.random.split(key, 32)
    f32 = jnp.float32
    L = DEPTH

    def w(k, shape, fan_in):
        return jax.random.normal(k, shape, f32) * (fan_in ** -0.5)

    def gain(k, shape):
        return 1.0 + 0.05 * jax.random.normal(k, shape, f32)

    x = jax.random.normal(ks[0], (BATCH, SEQ, D_MODEL), f32)
    mem = jax.random.normal(ks[1], (BATCH, N_MEM, D_MODEL), f32)
    positions = jnp.broadcast_to(jnp.arange(SEQ, dtype=jnp.int32)[None, :], (BATCH, SEQ))
    return {
        "x": x,
        "mem": mem,
        "positions": positions,
        "g_mix": gain(ks[2], (L, D_MODEL)),
        "w_in": w(ks[3], (L, D_MODEL, IN_WIDTH), D_MODEL),
        "b_gate": 0.01 * jax.random.normal(ks[4], (L, 2 * D_MODEL), f32),
        "g_q_lat": gain(ks[5], (L, MLA_Q_RANK)),
        "w_uq": w(ks[6], (L, MLA_Q_RANK, MLA_HEADS * (MLA_NOPE + MLA_ROPE)), MLA_Q_RANK),
        "g_kv_lat": gain(ks[7], (L, MLA_KV_RANK)),
        "w_ukv": w(ks[8], (L, MLA_KV_RANK, MLA_HEADS * (MLA_NOPE + MLA_V)), MLA_KV_RANK),
        "w_proj_mla": w(ks[9], (L, MLA_WIDTH, D_MODEL), MLA_WIDTH),
        "g_ret": gain(ks[10], (L, RET_WIDTH)),
        "w_proj_ret": w(ks[11], (L, RET_WIDTH, D_MODEL), RET_WIDTH),
        "w_out": w(ks[12], (L, D_MODEL, D_MODEL), D_MODEL),
        "g_cross": gain(ks[13], (L, D_MODEL)),
        "g_mem": gain(ks[14], (L, D_MODEL)),
        "w_xq": w(ks[15], (L, D_MODEL, D_MODEL), D_MODEL),
        "w_xkv": w(ks[16], (L, D_MODEL, 2 * D_MODEL), D_MODEL),
        "w_xo": w(ks[17], (L, D_MODEL, D_MODEL), D_MODEL),
        "g_ffn": gain(ks[18], (L, D_MODEL)),
        "w_up": w(ks[19], (L, D_MODEL, 2 * D_FF), D_MODEL),
        "w_conv": w(ks[20], (L, CONV_W, 2 * D_FF), CONV_W),
        "b_conv": 0.01 * jax.random.normal(ks[21], (L, 2 * D_FF), f32),
        "w_down": w(ks[22], (L, D_FF, D_MODEL), D_FF),
        "g_final": gain(ks[23], (D_MODEL,)),
    }


def reference(x, mem, positions, g_mix, w_in, b_gate, g_q_lat, w_uq, g_kv_lat, w_ukv, w_proj_mla,
              g_ret, w_proj_ret, w_out, g_cross, g_mem, w_xq, w_xkv, w_xo, g_ffn, w_up, w_conv,
              b_conv, w_down, g_final):
    B, S, D = x.shape
    M = mem.shape[1]
    sizes = (MLA_Q_RANK, MLA_KV_RANK, MLA_ROPE, RET_HEADS * RET_DK, RET_HEADS * RET_DK,
             RET_WIDTH, RET_WIDTH, 2 * D_MODEL)
    cuts = []
    acc = 0
    for sz in sizes[:-1]:
        acc += sz
        cuts.append(acc)
    h = x
    for l in range(DEPTH):
        u = rms_norm(h, g_mix[l])
        z = u @ w_in[l]
        c_q, c_kv, k_r, rq, rk, rv, rg, gates = jnp.split(z, cuts, axis=-1)

        q = (rms_norm(c_q, g_q_lat[l]) @ w_uq[l]).reshape(B, S, MLA_HEADS, MLA_NOPE + MLA_ROPE)
        q_nope, q_rope = q[..., :MLA_NOPE], rotary(q[..., MLA_NOPE:], positions)
        kv = (rms_norm(c_kv, g_kv_lat[l]) @ w_ukv[l]).reshape(B, S, MLA_HEADS, MLA_NOPE + MLA_V)
        k_nope, v_a = kv[..., :MLA_NOPE], kv[..., MLA_NOPE:]
        k_rope = jnp.broadcast_to(rotary(k_r[:, :, None, :], positions), (B, S, MLA_HEADS, MLA_ROPE))
        q_full = jnp.concatenate([q_nope, q_rope], axis=-1).transpose(0, 2, 1, 3)
        k_full = jnp.concatenate([k_nope, k_rope], axis=-1).transpose(0, 2, 1, 3)
        o_a = causal_block_attention(q_full, k_full, v_a.transpose(0, 2, 1, 3),
                                     (MLA_NOPE + MLA_ROPE) ** -0.5)
        y_a = o_a.transpose(0, 2, 1, 3).reshape(B, S, MLA_WIDTH) @ w_proj_mla[l]

        rq = rotary(rq.reshape(B, S, RET_HEADS, RET_DK), positions)
        rk = rotary(rk.reshape(B, S, RET_HEADS, RET_DK), positions) * (RET_DK ** -0.5)
        rv = rv.reshape(B, S, RET_HEADS, RET_DV)
        y_ret = head_group_norm(retention_chunkwise(rq, rk, rv), g_ret[l])
        y_r = (jax.nn.silu(rg) * y_ret) @ w_proj_ret[l]

        g_a, g_r = jnp.split(jax.nn.sigmoid(gates + b_gate[l]), 2, axis=-1)
        h = h + (g_a * y_a + g_r * y_r) @ w_out[l]

        xq = (rms_norm(h, g_cross[l]) @ w_xq[l]).reshape(B, S, X_HEADS, X_HEAD_DIM)
        mkv = (rms_norm(mem, g_mem[l]) @ w_xkv[l]).reshape(B, M, 2, X_HEADS, X_HEAD_DIM)
        mk, mv = mkv[:, :, 0], mkv[:, :, 1]
        s = jnp.einsum('bshd,bmhd->bhsm', xq, mk).astype(jnp.float32) * (X_HEAD_DIM ** -0.5)
        p = jax.nn.softmax(s, axis=-1).astype(mv.dtype)
        xo = jnp.einsum('bhsm,bmhd->bshd', p, mv).reshape(B, S, D)
        h = h + xo @ w_xo[l]

        up = causal_dwconv(rms_norm(h, g_ffn[l]) @ w_up[l], w_conv[l], b_conv[l])
        a, b = up[..., :D_FF], up[..., D_FF:]
        h = h + (jax.nn.silu(a) * b) @ w_down[l]
    return rms_norm(h, g_final)
```

```python
import functools

import numpy as np
import jax
import jax.numpy as jnp
from jax import lax
from jax.experimental import pallas as pl
from jax.experimental.pallas import tpu as pltpu

MLA_HEADS = 8
MLA_NOPE = 64
MLA_ROPE = 32
MLA_V = 64
MLA_Q_RANK = 256
MLA_KV_RANK = 128
RET_HEADS = 4
RET_DK = 128
RET_DV = 128
RET_CHUNK = 128
X_HEADS = 4
CONV_W = 3
ROPE_THETA = 10000.0
EPS = 1e-6

LANES = 128
SUBLANES = 8
HEAD_PAD = 128
FF_CHUNK = 256
VMEM_LIMIT = 56 * 1024 * 1024

BF16 = jnp.bfloat16
F32 = jnp.float32


def _rms(x, g):
    return x * lax.rsqrt(jnp.mean(x * x, axis=-1, keepdims=True) + EPS) * g


def _dot(a, b):
    return jnp.dot(a, b, preferred_element_type=F32)


def _dot_nt(a, b):
    return lax.dot_general(a, b, (((1,), (1,)), ((), ())), preferred_element_type=F32)


def _dot_tn(a, b):
    return lax.dot_general(a, b, (((0,), (0,)), ((), ())), preferred_element_type=F32)


def _const_spec(shape):
    nd = len(shape)
    return pl.BlockSpec(shape, lambda *_: (0,) * nd)


def _params(sem):
    return pltpu.CompilerParams(dimension_semantics=sem, vmem_limit_bytes=VMEM_LIMIT)


def _mem_kv_kernel(mem_ref, g_ref, w_ref, o_ref):
    n = _rms(mem_ref[...], g_ref[...]).astype(BF16)
    o_ref[...] = _dot(n, w_ref[...]).astype(o_ref.dtype)


def _mem_kv(mem, g_mem, w_xkv):
    B, M, D = mem.shape
    N = w_xkv.shape[1]
    return pl.pallas_call(
        _mem_kv_kernel,
        grid=(B,),
        in_specs=[pl.BlockSpec((None, M, D), lambda b: (b, 0, 0)),
                  _const_spec((1, D)), _const_spec((D, N))],
        out_specs=pl.BlockSpec((None, M, N), lambda b: (b, 0, 0)),
        out_shape=jax.ShapeDtypeStruct((B, M, N), BF16),
        compiler_params=_params(("parallel",)),
        name="mem_kv",
    )(mem, g_mem, w_xkv)


def _in_proj_kernel(x_ref, pos_ref, gmix_ref, wlat_ref, wret_ref, wgate_ref, bgate_ref,
                    gq_ref, wuq_ref, gkv_ref, wuk_ref, wuv_ref, inva_ref, invr_ref,
                    q_ref, k_ref, v_ref, rq_ref, rk_ref, rv_ref, rg_ref, gate_ref):
    u = _rms(x_ref[...], gmix_ref[...]).astype(BF16)
    pos = pos_ref[...].astype(F32)

    ang_a = pos * inva_ref[...]
    cos_a, sin_a = jnp.cos(ang_a), jnp.sin(ang_a)
    lane = lax.broadcasted_iota(jnp.int32, ang_a.shape, 1)
    first_half = jnp.logical_and(lane >= MLA_NOPE, lane < MLA_NOPE + MLA_ROPE // 2)

    def rot_a(blk):
        partner = jnp.where(first_half,
                            pltpu.roll(blk, HEAD_PAD - MLA_ROPE // 2, 1),
                            pltpu.roll(blk, MLA_ROPE // 2, 1))
        return blk * cos_a + partner * sin_a

    lat = _dot(u, wlat_ref[...])
    cq = _rms(lat[:, :MLA_Q_RANK], gq_ref[...]).astype(BF16)
    ckv = _rms(lat[:, MLA_Q_RANK:MLA_Q_RANK + MLA_KV_RANK], gkv_ref[...]).astype(BF16)
    kr = rot_a(lat[:, MLA_Q_RANK + MLA_KV_RANK:])

    qf = _dot(cq, wuq_ref[...])
    kf = _dot(ckv, wuk_ref[...])
    q_scale = (MLA_NOPE + MLA_ROPE) ** -0.5
    for h in range(MLA_HEADS):
        sl = slice(h * HEAD_PAD, (h + 1) * HEAD_PAD)
        q_ref[:, sl] = (rot_a(qf[:, sl]) * q_scale).astype(q_ref.dtype)
        k_ref[:, sl] = (kf[:, sl] + kr).astype(k_ref.dtype)
    v_ref[...] = _dot(ckv, wuv_ref[...]).astype(v_ref.dtype)

    ang_r = pos * invr_ref[...]
    cos_r, sin_r = jnp.cos(ang_r), jnp.sin(ang_r)
    ret = _dot(u, wret_ref[...])
    hw = RET_HEADS * RET_DK
    k_scale = RET_DK ** -0.5
    for h in range(RET_HEADS):
        sl = slice(h * RET_DK, (h + 1) * RET_DK)
        bq = ret[:, sl]
        rq_ref[:, sl] = (bq * cos_r + pltpu.roll(bq, RET_DK // 2, 1) * sin_r).astype(rq_ref.dtype)
        bk = ret[:, hw + h * RET_DK: hw + (h + 1) * RET_DK]
        rk_ref[:, sl] = ((bk * cos_r + pltpu.roll(bk, RET_DK // 2, 1) * sin_r) * k_scale).astype(rk_ref.dtype)
    rv_ref[...] = ret[:, 2 * hw:3 * hw].astype(rv_ref.dtype)
    rg = ret[:, 3 * hw:]
    rg_ref[...] = (rg * jax.nn.sigmoid(rg)).astype(rg_ref.dtype)

    gate_ref[...] = jax.nn.sigmoid(_dot(u, wgate_ref[...]) + bgate_ref[...]).astype(gate_ref.dtype)


def _in_proj(x2, pos2, g_mix, w_lat, w_ret, w_gate, b_gate, g_q, w_uq, g_kv, w_uk, w_uv,
             inv_a, inv_r, tm):
    T, D = x2.shape
    hw = RET_HEADS * RET_DK
    row = lambda n: pl.BlockSpec((tm, n), lambda i: (i, 0))
    consts = [g_mix, w_lat, w_ret, w_gate, b_gate, g_q, w_uq, g_kv, w_uk, w_uv, inv_a, inv_r]
    out_widths = [MLA_HEADS * HEAD_PAD, MLA_HEADS * HEAD_PAD, MLA_HEADS * MLA_V, hw, hw, hw, hw, 2 * D]
    out_dtypes = [BF16, BF16, BF16, F32, F32, BF16, BF16, BF16]
    return pl.pallas_call(
        _in_proj_kernel,
        grid=(T // tm,),
        in_specs=[row(D), row(1)] + [_const_spec(c.shape) for c in consts],
        out_specs=[row(n) for n in out_widths],
        out_shape=[jax.ShapeDtypeStruct((T, n), dt) for n, dt in zip(out_widths, out_dtypes)],
        compiler_params=_params(("parallel",)),
        name="in_proj",
    )(x2, pos2, *consts)


def _mla_attn_kernel(q_ref, k_ref, v_ref, o_ref, m_ref, l_ref, acc_ref, *, tq, tk):
    i = pl.program_id(2)
    heads = q_ref.shape[1] // HEAD_PAD

    m_ref[...] = jnp.full(m_ref.shape, -jnp.inf, F32)
    l_ref[...] = jnp.zeros(l_ref.shape, F32)
    acc_ref[...] = jnp.zeros(acc_ref.shape, F32)

    def step(j, masked):
        k0 = pl.multiple_of(j * tk, tk)
        vv = v_ref[pl.ds(k0, tk), :]
        for h in range(heads):
            sl = slice(h * HEAD_PAD, (h + 1) * HEAD_PAD)
            s = _dot_nt(q_ref[:, sl], k_ref[pl.ds(k0, tk), sl])
            if masked:
                qpos = i * tq + lax.broadcasted_iota(jnp.int32, s.shape, 0)
                kpos = j * tk + lax.broadcasted_iota(jnp.int32, s.shape, 1)
                s = jnp.where(kpos <= qpos, s, -jnp.inf)
            m_old = m_ref[h]
            m_new = jnp.maximum(m_old, jnp.max(s, axis=-1, keepdims=True))
            alpha = jnp.exp(m_old - m_new)
            p = jnp.exp(s - m_new)
            l_ref[h] = alpha * l_ref[h] + jnp.sum(p, axis=-1, keepdims=True)
            acc_ref[h] = alpha * acc_ref[h] + _dot(p.astype(BF16), vv)
            m_ref[h] = m_new

    n_full = (i * tq) // tk
    lax.fori_loop(0, n_full, lambda j, c: (step(j, False), c)[1], 0)
    for d in range(tq // tk):
        step(n_full + d, True)

    lane = lax.broadcasted_iota(jnp.int32, (tq, heads * MLA_V), 1)
    out = acc_ref[0] / l_ref[0]
    for h in range(1, heads):
        out = jnp.where(lane >= h * MLA_V, acc_ref[h] / l_ref[h], out)
    o_ref[...] = out.astype(o_ref.dtype)


def _mla_attn(q, k, v, tq, tk, heads_per_step=2):
    B, S, _ = q.shape
    groups = MLA_HEADS // heads_per_step
    qw = heads_per_step * HEAD_PAD
    vw = heads_per_step * MLA_V
    return pl.pallas_call(
        functools.partial(_mla_attn_kernel, tq=tq, tk=tk),
        grid=(B, groups, S // tq),
        in_specs=[pl.BlockSpec((None, tq, qw), lambda b, g, i: (b, i, g)),
                  pl.BlockSpec((None, S, qw), lambda b, g, i: (b, 0, g)),
                  pl.BlockSpec((None, S, vw), lambda b, g, i: (b, 0, g))],
        out_specs=pl.BlockSpec((None, tq, vw), lambda b, g, i: (b, i, g)),
        out_shape=jax.ShapeDtypeStruct((B, S, MLA_HEADS * MLA_V), BF16),
        scratch_shapes=[pltpu.VMEM((heads_per_step, tq, 1), F32),
                        pltpu.VMEM((heads_per_step, tq, 1), F32),
                        pltpu.VMEM((heads_per_step, tq, vw), F32)],
        compiler_params=_params(("parallel", "parallel", "parallel")),
        name="mla_attn",
    )(q, k, v)


def _retention_tables():
    C = RET_CHUNK
    log_g = np.log1p(-np.exp2(-5.0 - np.arange(RET_HEADS, dtype=np.float64)))
    idx = np.arange(C, dtype=np.float64)
    rel = idx[:, None] - idx[None, :]
    dmask = np.where(rel >= 0, np.exp(log_g[:, None, None] * np.maximum(rel, 0.0)), 0.0)
    zeta = np.exp(log_g[:, None] * (C - 1.0 - idx)[None, :])
    xi = np.exp(log_g[:, None] * (idx + 1.0)[None, :])
    decay = np.exp(log_g * C)
    f = lambda a: jnp.asarray(a, F32)
    return f(dmask), f(zeta[:, :, None]), f(xi[:, :, None]), f(np.broadcast_to(decay[:, None, None], (RET_HEADS, 1, LANES)))


def _retention_kernel(q_ref, k_ref, v_ref, rg_ref, gret_ref, dmask_ref, zeta_ref, xi_ref, decay_ref,
                      o_ref, state_ref, *, chunks):
    @pl.when(pl.program_id(1) == 0)
    def _():
        state_ref[...] = jnp.zeros(state_ref.shape, F32)

    C = RET_CHUNK
    for h in range(RET_HEADS):
        sl = slice(h * RET_DK, (h + 1) * RET_DK)
        dmask, zeta, xi = dmask_ref[h], zeta_ref[h], xi_ref[h]
        decay = decay_ref[h][:, :RET_DV]
        for c in range(chunks):
            rows = slice(c * C, (c + 1) * C)
            q = q_ref[rows, sl]
            k = k_ref[rows, sl]
            v = v_ref[rows, sl]
            s = (_dot_nt(q.astype(BF16), k.astype(BF16)) * dmask).astype(BF16)
            state = state_ref[h]
            y = _dot(s, v) + _dot((q * xi).astype(BF16), state.astype(BF16))
            state_ref[h] = state * decay + _dot_tn((k * zeta).astype(BF16), v)
            mu = jnp.mean(y, axis=-1, keepdims=True)
            yc = y - mu
            var = jnp.mean(yc * yc, axis=-1, keepdims=True)
            yn = yc * lax.rsqrt(var + EPS) * gret_ref[:, sl]
            o_ref[rows, sl] = (rg_ref[rows, sl].astype(F32) * yn).astype(o_ref.dtype)


def _retention(rq, rk, rv, rg, g_ret, tr):
    B, S, W = rq.shape
    tables = _retention_tables()
    blk = pl.BlockSpec((None, tr, W), lambda b, i: (b, i, 0))
    return pl.pallas_call(
        functools.partial(_retention_kernel, chunks=tr // RET_CHUNK),
        grid=(B, S // tr),
        in_specs=[blk, blk, blk, blk, _const_spec(g_ret.shape)] + [_const_spec(t.shape) for t in tables],
        out_specs=blk,
        out_shape=jax.ShapeDtypeStruct((B, S, W), BF16),
        scratch_shapes=[pltpu.VMEM((RET_HEADS, RET_DK, RET_DV), F32)],
        compiler_params=_params(("parallel", "arbitrary")),
        name="retention",
    )(rq, rk, rv, rg, g_ret, *tables)


def _mix_cross_kernel(x_ref, oa_ref, yr_ref, gate_ref, wpa_ref, wpr_ref, wout_ref, gc_ref, wxq_ref,
                      mkv_ref, wxo_ref, h_ref):
    D = x_ref.shape[1]
    hd = D // X_HEADS
    y_a = _dot(oa_ref[...], wpa_ref[...])
    y_r = _dot(yr_ref[...], wpr_ref[...])
    merged = gate_ref[:, :D].astype(F32) * y_a + gate_ref[:, D:].astype(F32) * y_r
    h1 = x_ref[...] + _dot(merged.astype(BF16), wout_ref[...])

    xq = _dot(_rms(h1, gc_ref[...]).astype(BF16), wxq_ref[...]).astype(BF16)
    scale = hd ** -0.5
    outs = []
    for h in range(X_HEADS):
        sl = slice(h * hd, (h + 1) * hd)
        s = _dot_nt(xq[:, sl], mkv_ref[:, sl]) * scale
        e = jnp.exp(s - jnp.max(s, axis=-1, keepdims=True))
        o = _dot(e.astype(BF16), mkv_ref[:, D + h * hd: D + (h + 1) * hd])
        outs.append((o / jnp.sum(e, axis=-1, keepdims=True)).astype(BF16))
    xo = jnp.concatenate(outs, axis=-1)
    h_ref[...] = h1 + _dot(xo, wxo_ref[...])


def _mix_cross(x2, o_a, y_rg, gates, w_pa, w_pr, w_out, g_cross, w_xq, mkv, w_xo, tm, seq):
    T, D = x2.shape
    M = mkv.shape[1]
    row = lambda n: pl.BlockSpec((tm, n), lambda i: (i, 0))
    steps_per_batch = seq // tm
    return pl.pallas_call(
        _mix_cross_kernel,
        grid=(T // tm,),
        in_specs=[row(D), row(o_a.shape[1]), row(y_rg.shape[1]), row(gates.shape[1]),
                  _const_spec(w_pa.shape), _const_spec(w_pr.shape), _const_spec(w_out.shape),
                  _const_spec(g_cross.shape), _const_spec(w_xq.shape),
                  pl.BlockSpec((None, M, 2 * D), lambda i: (i // steps_per_batch, 0, 0)),
                  _const_spec(w_xo.shape)],
        out_specs=row(D),
        out_shape=jax.ShapeDtypeStruct((T, D), F32),
        compiler_params=_params(("parallel",)),
        name="mix_cross",
    )(x2, o_a, y_rg, gates, w_pa, w_pr, w_out, g_cross, w_xq, mkv, w_xo)


def _ffn_kernel(h_ref, gf_ref, wua_ref, wub_ref, wca_ref, wcb_ref, bca_ref, bcb_ref, wd_ref, gfin_ref,
                o_ref, n_ref, acc_ref, bufa_ref, bufb_ref, haloa_ref, halob_ref, *, steps_per_batch):
    tm = h_ref.shape[0]
    halo = SUBLANES

    @pl.when(pl.program_id(0) % steps_per_batch == 0)
    def _():
        haloa_ref[...] = jnp.zeros(haloa_ref.shape, F32)
        halob_ref[...] = jnp.zeros(halob_ref.shape, F32)

    n_ref[...] = _rms(h_ref[...], gf_ref[...]).astype(BF16)
    acc_ref[...] = jnp.zeros(acc_ref.shape, F32)

    def conv(buf_ref, halo_ref, w_ref, b_ref, up, j):
        buf_ref[:halo, :] = halo_ref[j]
        buf_ref[halo:, :] = up
        halo_ref[j] = up[tm - halo:, :]
        w = w_ref[j]
        out = up * w[CONV_W - 1:CONV_W, :] + b_ref[j]
        for t in range(1, CONV_W):
            out = out + buf_ref[halo - t: halo - t + tm, :] * w[CONV_W - 1 - t:CONV_W - t, :]
        return out

    def body(j, carry):
        n = n_ref[...]
        a = conv(bufa_ref, haloa_ref, wca_ref, bca_ref, _dot(n, wua_ref[j]), j)
        b = conv(bufb_ref, halob_ref, wcb_ref, bcb_ref, _dot(n, wub_ref[j]), j)
        act = (a * jax.nn.sigmoid(a) * b).astype(BF16)
        acc_ref[...] += _dot(act, wd_ref[j])
        return carry

    lax.fori_loop(0, wua_ref.shape[0], body, 0)
    o_ref[...] = _rms(h_ref[...] + acc_ref[...], gfin_ref[...])


def _ffn_out(h2, g_ffn, w_ua, w_ub, w_ca, w_cb, b_ca, b_cb, w_d, g_final, tm, seq):
    T, D = h2.shape
    nch, _, fc = w_ua.shape
    row = pl.BlockSpec((tm, D), lambda i: (i, 0))
    consts = [g_ffn, w_ua, w_ub, w_ca, w_cb, b_ca, b_cb, w_d, g_final]
    return pl.pallas_call(
        functools.partial(_ffn_kernel, steps_per_batch=seq // tm),
        grid=(T // tm,),
        in_specs=[row] + [_const_spec(c.shape) for c in consts],
        out_specs=row,
        out_shape=jax.ShapeDtypeStruct((T, D), F32),
        scratch_shapes=[pltpu.VMEM((tm, D), BF16), pltpu.VMEM((tm, D), F32),
                        pltpu.VMEM((tm + SUBLANES, fc), F32), pltpu.VMEM((tm + SUBLANES, fc), F32),
                        pltpu.VMEM((nch, SUBLANES, fc), F32), pltpu.VMEM((nch, SUBLANES, fc), F32)],
        compiler_params=_params(("arbitrary",)),
        name="ffn_out",
    )(h2, *consts)


def _rope_inv(half):
    return ROPE_THETA ** (-np.arange(half, dtype=np.float32) / np.float32(half))


def _mla_column_maps():
    half = MLA_ROPE // 2
    used = MLA_NOPE + MLA_ROPE
    q_cols = np.zeros((MLA_HEADS, HEAD_PAD), np.int32)
    k_cols = np.zeros((MLA_HEADS, HEAD_PAD), np.int32)
    valid_q = np.zeros((MLA_HEADS, HEAD_PAD), bool)
    valid_k = np.zeros((MLA_HEADS, HEAD_PAD), bool)
    for h in range(MLA_HEADS):
        q_cols[h, :used] = h * used + np.arange(used)
        valid_q[h, :used] = True
        k_cols[h, :MLA_NOPE] = h * (MLA_NOPE + MLA_V) + np.arange(MLA_NOPE)
        valid_k[h, :MLA_NOPE] = True
    v_cols = (np.arange(MLA_HEADS)[:, None] * (MLA_NOPE + MLA_V) + MLA_NOPE + np.arange(MLA_V)[None, :])
    del half
    return (q_cols.reshape(-1), valid_q.reshape(-1), k_cols.reshape(-1), valid_k.reshape(-1),
            v_cols.reshape(-1).astype(np.int32))


def kernel(x, mem, positions, g_mix, w_in, b_gate, g_q_lat, w_uq, g_kv_lat, w_ukv, w_proj_mla, g_ret,
           w_proj_ret, w_out, g_cross, g_mem, w_xq, w_xkv, w_xo, g_ffn, w_up, w_conv, b_conv, w_down, g_final):
    B, S, D = x.shape
    T = B * S
    depth = w_in.shape[0]
    d_ff = w_down.shape[1]
    hw = RET_HEADS * RET_DK
    tm = min(256, S)
    tq = min(512, S)
    tr = min(512, S)

    half_a = MLA_ROPE // 2
    inv_a = np.zeros((1, HEAD_PAD), np.float32)
    inv_a[0, MLA_NOPE:MLA_NOPE + half_a] = -_rope_inv(half_a)
    inv_a[0, MLA_NOPE + half_a:MLA_NOPE + MLA_ROPE] = _rope_inv(half_a)
    inv_r = np.concatenate([-_rope_inv(RET_DK // 2), _rope_inv(RET_DK // 2)])[None, :]
    q_cols, valid_q, k_cols, valid_k, v_cols = _mla_column_maps()

    h = x.reshape(T, D)
    pos2 = positions.reshape(T, 1)
    row2 = lambda a: a.reshape(1, -1)
    for l in range(depth):
        c0 = MLA_Q_RANK + MLA_KV_RANK
        w_kr = w_in[l][:, c0:c0 + MLA_ROPE]
        kr_pad = jnp.zeros((D, HEAD_PAD), F32).at[:, MLA_NOPE:MLA_NOPE + MLA_ROPE].set(w_kr)
        w_lat = jnp.concatenate([w_in[l][:, :c0], kr_pad], axis=1).astype(BF16)
        c1 = c0 + MLA_ROPE
        w_ret = w_in[l][:, c1:c1 + 4 * hw].astype(BF16)
        w_gate = w_in[l][:, c1 + 4 * hw:].astype(BF16)
        w_uq_p = jnp.where(valid_q[None, :], w_uq[l][:, q_cols], 0.0).astype(BF16)
        w_uk_p = jnp.where(valid_k[None, :], w_ukv[l][:, k_cols], 0.0).astype(BF16)
        w_uv = w_ukv[l][:, v_cols].astype(BF16)
        nch = d_ff // FF_CHUNK
        chunked = lambda w: w.reshape(w.shape[0], nch, FF_CHUNK).transpose(1, 0, 2)
        w_ua = chunked(w_up[l][:, :d_ff]).astype(BF16)
        w_ub = chunked(w_up[l][:, d_ff:]).astype(BF16)
        w_ca, w_cb = chunked(w_conv[l][:, :d_ff]), chunked(w_conv[l][:, d_ff:])
        b_ca, b_cb = chunked(b_conv[l][None, :d_ff]), chunked(b_conv[l][None, d_ff:])
        w_d = w_down[l].reshape(nch, FF_CHUNK, D).astype(BF16)

        mkv = _mem_kv(mem, row2(g_mem[l]), w_xkv[l].astype(BF16))
        q, k, v, rq, rk, rv, rg, gates = _in_proj(
            h, pos2, row2(g_mix[l]), w_lat, w_ret, w_gate, row2(b_gate[l]), row2(g_q_lat[l]), w_uq_p,
            row2(g_kv_lat[l]), w_uk_p, w_uv, jnp.asarray(inv_a), jnp.asarray(inv_r), tm)
        seq3 = lambda a: a.reshape(B, S, a.shape[-1])
        o_a = _mla_attn(seq3(q), seq3(k), seq3(v), tq, tq)
        y_rg = _retention(seq3(rq), seq3(rk), seq3(rv), seq3(rg), row2(g_ret[l]), tr)
        h = _mix_cross(h, o_a.reshape(T, -1), y_rg.reshape(T, -1), gates, w_proj_mla[l].astype(BF16),
                       w_proj_ret[l].astype(BF16), w_out[l].astype(BF16), row2(g_cross[l]),
                       w_xq[l].astype(BF16), mkv, w_xo[l].astype(BF16), tm, S)
        last = l == depth - 1
        assert last, "only a single layer stack is supported"
        h = _ffn_out(h, row2(g_ffn[l]), w_ua, w_ub, w_ca, w_cb, b_ca, b_cb, w_d, row2(g_final), tm, S)
    return h.reshape(B, S, D)
```

```python
import functools

import numpy as np
import jax
import jax.numpy as jnp
from jax import lax
from jax.experimental import pallas as pl
from jax.experimental.pallas import tpu as pltpu

MLA_HEADS = 8
MLA_NOPE = 64
MLA_ROPE = 32
MLA_V = 64
MLA_Q_RANK = 256
MLA_KV_RANK = 128
RET_HEADS = 4
RET_DK = 128
RET_DV = 128
RET_CHUNK = 128
X_HEADS = 4
CONV_W = 3
ROPE_THETA = 10000.0
EPS = 1e-6

LANES = 128
SUBLANES = 8
HEAD_PAD = 128
FF_CHUNK = 256
VMEM_LIMIT = 56 * 1024 * 1024

BF16 = jnp.bfloat16
F32 = jnp.float32


def _rms(x, g):
    return x * lax.rsqrt(jnp.mean(x * x, axis=-1, keepdims=True) + EPS) * g


def _dot(a, b):
    return jnp.dot(a, b, preferred_element_type=F32)


def _dot_nt(a, b):
    return lax.dot_general(a, b, (((1,), (1,)), ((), ())), preferred_element_type=F32)


def _dot_tn(a, b):
    return lax.dot_general(a, b, (((0,), (0,)), ((), ())), preferred_element_type=F32)


def _const_spec(shape):
    nd = len(shape)
    return pl.BlockSpec(shape, lambda *_: (0,) * nd)


def _params(sem):
    return pltpu.CompilerParams(dimension_semantics=sem, vmem_limit_bytes=VMEM_LIMIT)


def _mem_kv_kernel(mem_ref, g_ref, w_ref, o_ref):
    n = _rms(mem_ref[...], g_ref[...]).astype(BF16)
    o_ref[...] = _dot(n, w_ref[...]).astype(o_ref.dtype)


def _mem_kv(mem, g_mem, w_xkv):
    B, M, D = mem.shape
    N = w_xkv.shape[1]
    return pl.pallas_call(
        _mem_kv_kernel,
        grid=(B,),
        in_specs=[pl.BlockSpec((None, M, D), lambda b: (b, 0, 0)),
                  _const_spec((1, D)), _const_spec((D, N))],
        out_specs=pl.BlockSpec((None, M, N), lambda b: (b, 0, 0)),
        out_shape=jax.ShapeDtypeStruct((B, M, N), BF16),
        compiler_params=_params(("parallel",)),
        name="mem_kv",
    )(mem, g_mem, w_xkv)


def _in_proj_kernel(x_ref, pos_ref, gmix_ref, wlat_ref, wret_ref, wgate_ref, bgate_ref,
                    gq_ref, wuq_ref, gkv_ref, wuk_ref, wuv_ref, inva_ref, invr_ref,
                    q_ref, k_ref, vt_ref, rq_ref, rk_ref, rv_ref, rg_ref, gate_ref):
    u = _rms(x_ref[...], gmix_ref[...]).astype(BF16)
    pos = pos_ref[...].astype(F32)

    ang_a = pos * inva_ref[...]
    cos_a, sin_a = jnp.cos(ang_a), jnp.sin(ang_a)
    lane = lax.broadcasted_iota(jnp.int32, ang_a.shape, 1)
    first_half = jnp.logical_and(lane >= MLA_NOPE, lane < MLA_NOPE + MLA_ROPE // 2)

    def rot_a(blk):
        partner = jnp.where(first_half,
                            pltpu.roll(blk, HEAD_PAD - MLA_ROPE // 2, 1),
                            pltpu.roll(blk, MLA_ROPE // 2, 1))
        return blk * cos_a + partner * sin_a

    lat = _dot(u, wlat_ref[...])
    cq = _rms(lat[:, :MLA_Q_RANK], gq_ref[...]).astype(BF16)
    ckv = _rms(lat[:, MLA_Q_RANK:MLA_Q_RANK + MLA_KV_RANK], gkv_ref[...]).astype(BF16)
    kr = rot_a(lat[:, MLA_Q_RANK + MLA_KV_RANK:])

    qf = _dot(cq, wuq_ref[...])
    kf = _dot(ckv, wuk_ref[...])
    q_scale = (MLA_NOPE + MLA_ROPE) ** -0.5 * np.log2(np.e)
    for h in range(MLA_HEADS):
        sl = slice(h * HEAD_PAD, (h + 1) * HEAD_PAD)
        q_ref[:, sl] = (rot_a(qf[:, sl]) * q_scale).astype(q_ref.dtype)
        k_ref[:, sl] = (kf[:, sl] + kr).astype(k_ref.dtype)
    vt_ref[...] = _dot_nt(wuv_ref[...], ckv).astype(vt_ref.dtype)

    ang_r = pos * invr_ref[...]
    cos_r, sin_r = jnp.cos(ang_r), jnp.sin(ang_r)
    ret = _dot(u, wret_ref[...])
    hw = RET_HEADS * RET_DK
    k_scale = RET_DK ** -0.5
    for h in range(RET_HEADS):
        sl = slice(h * RET_DK, (h + 1) * RET_DK)
        bq = ret[:, sl]
        rq_ref[:, sl] = (bq * cos_r + pltpu.roll(bq, RET_DK // 2, 1) * sin_r).astype(rq_ref.dtype)
        bk = ret[:, hw + h * RET_DK: hw + (h + 1) * RET_DK]
        rk_ref[:, sl] = ((bk * cos_r + pltpu.roll(bk, RET_DK // 2, 1) * sin_r) * k_scale).astype(rk_ref.dtype)
    rv_ref[...] = ret[:, 2 * hw:3 * hw].astype(rv_ref.dtype)
    rg = ret[:, 3 * hw:]
    rg_ref[...] = (rg * jax.nn.sigmoid(rg)).astype(rg_ref.dtype)

    gate_ref[...] = jax.nn.sigmoid(_dot(u, wgate_ref[...]) + bgate_ref[...]).astype(gate_ref.dtype)


def _in_proj(x2, pos2, g_mix, w_lat, w_ret, w_gate, b_gate, g_q, w_uq, g_kv, w_uk, w_uv,
             inv_a, inv_r, tm, tkv):
    T, D = x2.shape
    hw = RET_HEADS * RET_DK
    row = lambda n: pl.BlockSpec((tm, n), lambda i: (i, 0))
    consts = [g_mix, w_lat, w_ret, w_gate, b_gate, g_q, w_uq, g_kv, w_uk, w_uv, inv_a, inv_r]
    out_widths = [MLA_HEADS * HEAD_PAD, MLA_HEADS * HEAD_PAD, hw, hw, hw, hw, 2 * D]
    out_dtypes = [BF16, BF16, F32, F32, BF16, BF16, BF16]
    vw = MLA_HEADS * MLA_V
    per = tkv // tm
    vt_spec = pl.BlockSpec((None, vw, tm), lambda i: (i // per, 0, i % per))
    vt_shape = jax.ShapeDtypeStruct((T // tkv, vw, tkv), BF16)
    out_specs = [row(n) for n in out_widths]
    out_shape = [jax.ShapeDtypeStruct((T, n), dt) for n, dt in zip(out_widths, out_dtypes)]
    return pl.pallas_call(
        _in_proj_kernel,
        grid=(T // tm,),
        in_specs=[row(D), row(1)] + [_const_spec(c.shape) for c in consts],
        out_specs=out_specs[:2] + [vt_spec] + out_specs[2:],
        out_shape=out_shape[:2] + [vt_shape] + out_shape[2:],
        compiler_params=_params(("parallel",)),
        name="in_proj",
    )(x2, pos2, *consts)


def _mla_attn_kernel(q_ref, k_ref, vt_ref, o_ref, m_ref, l_ref, acc_ref, *, tq, tk):
    i = pl.program_id(2)
    heads = q_ref.shape[1] // HEAD_PAD

    m_ref[...] = jnp.full(m_ref.shape, -jnp.inf, F32)
    l_ref[...] = jnp.zeros(l_ref.shape, F32)
    acc_ref[...] = jnp.zeros(acc_ref.shape, F32)

    def step(j, masked):
        k0 = pl.multiple_of(j * tk, tk)
        for h in range(heads):
            sl = slice(h * HEAD_PAD, (h + 1) * HEAD_PAD)
            s = _dot_nt(k_ref[pl.ds(k0, tk), sl], q_ref[:, sl])
            if masked:
                kpos = j * tk + lax.broadcasted_iota(jnp.int32, s.shape, 0)
                qpos = i * tq + lax.broadcasted_iota(jnp.int32, s.shape, 1)
                s = jnp.where(kpos <= qpos, s, -jnp.inf)
            m_old = m_ref[h]
            m_new = jnp.maximum(m_old, jnp.max(s, axis=0, keepdims=True))
            alpha = jnp.exp2(m_old - m_new)
            p = jnp.exp2(s - m_new)
            l_ref[h] = alpha * l_ref[h] + jnp.sum(p, axis=0, keepdims=True)
            vt = vt_ref[j, h * MLA_V:(h + 1) * MLA_V, :]
            acc_ref[h] = alpha * acc_ref[h] + _dot(vt, p.astype(BF16))
            m_ref[h] = m_new

    n_full = (i * tq) // tk
    lax.fori_loop(0, n_full, lambda j, c: (step(j, False), c)[1], 0)
    for d in range(tq // tk):
        step(n_full + d, True)

    out_t = jnp.concatenate([acc_ref[h] / l_ref[h] for h in range(heads)], axis=0)
    o_ref[...] = out_t.T.astype(o_ref.dtype)


def _mla_attn(q, k, vt, tq, tk, heads_per_step=2):
    B, S, _ = q.shape
    groups = MLA_HEADS // heads_per_step
    qw = heads_per_step * HEAD_PAD
    vw = heads_per_step * MLA_V
    return pl.pallas_call(
        functools.partial(_mla_attn_kernel, tq=tq, tk=tk),
        grid=(B, groups, S // tq),
        in_specs=[pl.BlockSpec((None, tq, qw), lambda b, g, i: (b, i, g)),
                  pl.BlockSpec((None, S, qw), lambda b, g, i: (b, 0, g)),
                  pl.BlockSpec((None, S // tk, vw, tk), lambda b, g, i: (b, 0, g, 0))],
        out_specs=pl.BlockSpec((None, tq, vw), lambda b, g, i: (b, i, g)),
        out_shape=jax.ShapeDtypeStruct((B, S, MLA_HEADS * MLA_V), BF16),
        scratch_shapes=[pltpu.VMEM((heads_per_step, 1, tq), F32),
                        pltpu.VMEM((heads_per_step, 1, tq), F32),
                        pltpu.VMEM((heads_per_step, MLA_V, tq), F32)],
        compiler_params=_params(("parallel", "parallel", "parallel")),
        name="mla_attn",
    )(q, k, vt)


def _retention_tables():
    C = RET_CHUNK
    log_g = np.log1p(-np.exp2(-5.0 - np.arange(RET_HEADS, dtype=np.float64)))
    idx = np.arange(C, dtype=np.float64)
    rel = idx[:, None] - idx[None, :]
    dmask = np.where(rel >= 0, np.exp(log_g[:, None, None] * np.maximum(rel, 0.0)), 0.0)
    zeta = np.exp(log_g[:, None] * (C - 1.0 - idx)[None, :])
    xi = np.exp(log_g[:, None] * (idx + 1.0)[None, :])
    decay = np.exp(log_g * C)
    f = lambda a: jnp.asarray(a, F32)
    return f(dmask), f(zeta[:, :, None]), f(xi[:, :, None]), f(np.broadcast_to(decay[:, None, None], (RET_HEADS, 1, LANES)))


def _retention_kernel(q_ref, k_ref, v_ref, rg_ref, gret_ref, dmask_ref, zeta_ref, xi_ref, decay_ref,
                      o_ref, state_ref, *, chunks):
    @pl.when(pl.program_id(1) == 0)
    def _():
        state_ref[...] = jnp.zeros(state_ref.shape, F32)

    C = RET_CHUNK
    for h in range(RET_HEADS):
        sl = slice(h * RET_DK, (h + 1) * RET_DK)
        dmask, zeta, xi = dmask_ref[h], zeta_ref[h], xi_ref[h]
        decay = decay_ref[h][:, :RET_DV]
        for c in range(chunks):
            rows = slice(c * C, (c + 1) * C)
            q = q_ref[rows, sl]
            k = k_ref[rows, sl]
            v = v_ref[rows, sl]
            s = (_dot_nt(q.astype(BF16), k.astype(BF16)) * dmask).astype(BF16)
            state = state_ref[h]
            y = _dot(s, v) + _dot((q * xi).astype(BF16), state.astype(BF16))
            state_ref[h] = state * decay + _dot_tn((k * zeta).astype(BF16), v)
            mu = jnp.mean(y, axis=-1, keepdims=True)
            yc = y - mu
            var = jnp.mean(yc * yc, axis=-1, keepdims=True)
            yn = yc * lax.rsqrt(var + EPS) * gret_ref[:, sl]
            o_ref[rows, sl] = (rg_ref[rows, sl].astype(F32) * yn).astype(o_ref.dtype)


def _retention(rq, rk, rv, rg, g_ret, tr):
    B, S, W = rq.shape
    tables = _retention_tables()
    blk = pl.BlockSpec((None, tr, W), lambda b, i: (b, i, 0))
    return pl.pallas_call(
        functools.partial(_retention_kernel, chunks=tr // RET_CHUNK),
        grid=(B, S // tr),
        in_specs=[blk, blk, blk, blk, _const_spec(g_ret.shape)] + [_const_spec(t.shape) for t in tables],
        out_specs=blk,
        out_shape=jax.ShapeDtypeStruct((B, S, W), BF16),
        scratch_shapes=[pltpu.VMEM((RET_HEADS, RET_DK, RET_DV), F32)],
        compiler_params=_params(("parallel", "arbitrary")),
        name="retention",
    )(rq, rk, rv, rg, g_ret, *tables)


def _mix_cross_kernel(x_ref, oa_ref, yr_ref, gate_ref, wpa_ref, wpr_ref, wout_ref, gc_ref, wxq_ref,
                      mkv_ref, wxo_ref, h_ref):
    D = x_ref.shape[1]
    hd = D // X_HEADS
    y_a = _dot(oa_ref[...], wpa_ref[...])
    y_r = _dot(yr_ref[...], wpr_ref[...])
    merged = gate_ref[:, :D].astype(F32) * y_a + gate_ref[:, D:].astype(F32) * y_r
    h1 = x_ref[...] + _dot(merged.astype(BF16), wout_ref[...])

    xq = _dot(_rms(h1, gc_ref[...]).astype(BF16), wxq_ref[...]).astype(BF16)
    scale = hd ** -0.5
    outs = []
    for h in range(X_HEADS):
        sl = slice(h * hd, (h + 1) * hd)
        s = _dot_nt(xq[:, sl], mkv_ref[:, sl]) * scale
        e = jnp.exp(s - jnp.max(s, axis=-1, keepdims=True))
        o = _dot(e.astype(BF16), mkv_ref[:, D + h * hd: D + (h + 1) * hd])
        outs.append((o / jnp.sum(e, axis=-1, keepdims=True)).astype(BF16))
    xo = jnp.concatenate(outs, axis=-1)
    h_ref[...] = h1 + _dot(xo, wxo_ref[...])


def _mix_cross(x2, o_a, y_rg, gates, w_pa, w_pr, w_out, g_cross, w_xq, mkv, w_xo, tm, seq):
    T, D = x2.shape
    M = mkv.shape[1]
    row = lambda n: pl.BlockSpec((tm, n), lambda i: (i, 0))
    steps_per_batch = seq // tm
    return pl.pallas_call(
        _mix_cross_kernel,
        grid=(T // tm,),
        in_specs=[row(D), row(o_a.shape[1]), row(y_rg.shape[1]), row(gates.shape[1]),
                  _const_spec(w_pa.shape), _const_spec(w_pr.shape), _const_spec(w_out.shape),
                  _const_spec(g_cross.shape), _const_spec(w_xq.shape),
                  pl.BlockSpec((None, M, 2 * D), lambda i: (i // steps_per_batch, 0, 0)),
                  _const_spec(w_xo.shape)],
        out_specs=row(D),
        out_shape=jax.ShapeDtypeStruct((T, D), F32),
        compiler_params=_params(("parallel",)),
        name="mix_cross",
    )(x2, o_a, y_rg, gates, w_pa, w_pr, w_out, g_cross, w_xq, mkv, w_xo)


def _ffn_kernel(h_ref, gf_ref, wua_ref, wub_ref, wca_ref, wcb_ref, bca_ref, bcb_ref, wd_ref, gfin_ref,
                o_ref, n_ref, acc_ref, bufa_ref, bufb_ref, haloa_ref, halob_ref, *, steps_per_batch):
    tm = h_ref.shape[0]
    halo = SUBLANES

    @pl.when(pl.program_id(0) % steps_per_batch == 0)
    def _():
        haloa_ref[...] = jnp.zeros(haloa_ref.shape, F32)
        halob_ref[...] = jnp.zeros(halob_ref.shape, F32)

    n_ref[...] = _rms(h_ref[...], gf_ref[...]).astype(BF16)
    acc_ref[...] = jnp.zeros(acc_ref.shape, F32)

    def conv(buf_ref, halo_ref, w_ref, b_ref, up, j):
        buf_ref[:halo, :] = halo_ref[j]
        buf_ref[halo:, :] = up
        halo_ref[j] = up[tm - halo:, :]
        w = w_ref[j]
        out = up * w[CONV_W - 1:CONV_W, :] + b_ref[j]
        for t in range(1, CONV_W):
            out = out + buf_ref[halo - t: halo - t + tm, :] * w[CONV_W - 1 - t:CONV_W - t, :]
        return out

    def body(j, carry):
        n = n_ref[...]
        a = conv(bufa_ref, haloa_ref, wca_ref, bca_ref, _dot(n, wua_ref[j]), j)
        b = conv(bufb_ref, halob_ref, wcb_ref, bcb_ref, _dot(n, wub_ref[j]), j)
        act = (a * jax.nn.sigmoid(a) * b).astype(BF16)
        acc_ref[...] += _dot(act, wd_ref[j])
        return carry

    lax.fori_loop(0, wua_ref.shape[0], body, 0)
    o_ref[...] = _rms(h_ref[...] + acc_ref[...], gfin_ref[...])


def _ffn_out(h2, g_ffn, w_ua, w_ub, w_ca, w_cb, b_ca, b_cb, w_d, g_final, tm, seq):
    T, D = h2.shape
    nch, _, fc = w_ua.shape
    row = pl.BlockSpec((tm, D), lambda i: (i, 0))
    consts = [g_ffn, w_ua, w_ub, w_ca, w_cb, b_ca, b_cb, w_d, g_final]
    return pl.pallas_call(
        functools.partial(_ffn_kernel, steps_per_batch=seq // tm),
        grid=(T // tm,),
        in_specs=[row] + [_const_spec(c.shape) for c in consts],
        out_specs=row,
        out_shape=jax.ShapeDtypeStruct((T, D), F32),
        scratch_shapes=[pltpu.VMEM((tm, D), BF16), pltpu.VMEM((tm, D), F32),
                        pltpu.VMEM((tm + SUBLANES, fc), F32), pltpu.VMEM((tm + SUBLANES, fc), F32),
                        pltpu.VMEM((nch, SUBLANES, fc), F32), pltpu.VMEM((nch, SUBLANES, fc), F32)],
        compiler_params=_params(("arbitrary",)),
        name="ffn_out",
    )(h2, *consts)


def _rope_inv(half):
    return ROPE_THETA ** (-np.arange(half, dtype=np.float32) / np.float32(half))


def _mla_column_maps():
    half = MLA_ROPE // 2
    used = MLA_NOPE + MLA_ROPE
    q_cols = np.zeros((MLA_HEADS, HEAD_PAD), np.int32)
    k_cols = np.zeros((MLA_HEADS, HEAD_PAD), np.int32)
    valid_q = np.zeros((MLA_HEADS, HEAD_PAD), bool)
    valid_k = np.zeros((MLA_HEADS, HEAD_PAD), bool)
    for h in range(MLA_HEADS):
        q_cols[h, :used] = h * used + np.arange(used)
        valid_q[h, :used] = True
        k_cols[h, :MLA_NOPE] = h * (MLA_NOPE + MLA_V) + np.arange(MLA_NOPE)
        valid_k[h, :MLA_NOPE] = True
    v_cols = (np.arange(MLA_HEADS)[:, None] * (MLA_NOPE + MLA_V) + MLA_NOPE + np.arange(MLA_V)[None, :])
    del half
    return (q_cols.reshape(-1), valid_q.reshape(-1), k_cols.reshape(-1), valid_k.reshape(-1),
            v_cols.reshape(-1).astype(np.int32))


def kernel(x, mem, positions, g_mix, w_in, b_gate, g_q_lat, w_uq, g_kv_lat, w_ukv, w_proj_mla, g_ret,
           w_proj_ret, w_out, g_cross, g_mem, w_xq, w_xkv, w_xo, g_ffn, w_up, w_conv, b_conv, w_down, g_final):
    B, S, D = x.shape
    T = B * S
    depth = w_in.shape[0]
    d_ff = w_down.shape[1]
    hw = RET_HEADS * RET_DK
    tm = min(256, S)
    tq = min(512, S)
    tr = min(512, S)

    half_a = MLA_ROPE // 2
    inv_a = np.zeros((1, HEAD_PAD), np.float32)
    inv_a[0, MLA_NOPE:MLA_NOPE + half_a] = -_rope_inv(half_a)
    inv_a[0, MLA_NOPE + half_a:MLA_NOPE + MLA_ROPE] = _rope_inv(half_a)
    inv_r = np.concatenate([-_rope_inv(RET_DK // 2), _rope_inv(RET_DK // 2)])[None, :]
    q_cols, valid_q, k_cols, valid_k, v_cols = _mla_column_maps()

    h = x.reshape(T, D)
    pos2 = positions.reshape(T, 1)
    row2 = lambda a: a.reshape(1, -1)
    for l in range(depth):
        c0 = MLA_Q_RANK + MLA_KV_RANK
        w_kr = w_in[l][:, c0:c0 + MLA_ROPE]
        kr_pad = jnp.zeros((D, HEAD_PAD), F32).at[:, MLA_NOPE:MLA_NOPE + MLA_ROPE].set(w_kr)
        w_lat = jnp.concatenate([w_in[l][:, :c0], kr_pad], axis=1).astype(BF16)
        c1 = c0 + MLA_ROPE
        w_ret = w_in[l][:, c1:c1 + 4 * hw].astype(BF16)
        w_gate = w_in[l][:, c1 + 4 * hw:].astype(BF16)
        w_uq_p = jnp.where(valid_q[None, :], w_uq[l][:, q_cols], 0.0).astype(BF16)
        w_uk_p = jnp.where(valid_k[None, :], w_ukv[l][:, k_cols], 0.0).astype(BF16)
        w_uv = w_ukv[l][:, v_cols].T.astype(BF16)
        nch = d_ff // FF_CHUNK
        chunked = lambda w: w.reshape(w.shape[0], nch, FF_CHUNK).transpose(1, 0, 2)
        w_ua = chunked(w_up[l][:, :d_ff]).astype(BF16)
        w_ub = chunked(w_up[l][:, d_ff:]).astype(BF16)
        w_ca, w_cb = chunked(w_conv[l][:, :d_ff]), chunked(w_conv[l][:, d_ff:])
        b_ca, b_cb = chunked(b_conv[l][None, :d_ff]), chunked(b_conv[l][None, d_ff:])
        w_d = w_down[l].reshape(nch, FF_CHUNK, D).astype(BF16)

        mkv = _mem_kv(mem, row2(g_mem[l]), w_xkv[l].astype(BF16))
        q, k, vt, rq, rk, rv, rg, gates = _in_proj(
            h, pos2, row2(g_mix[l]), w_lat, w_ret, w_gate, row2(b_gate[l]), row2(g_q_lat[l]), w_uq_p,
            row2(g_kv_lat[l]), w_uk_p, w_uv, jnp.asarray(inv_a), jnp.asarray(inv_r), tm, tq)
        seq3 = lambda a: a.reshape(B, S, a.shape[-1])
        o_a = _mla_attn(seq3(q), seq3(k), vt.reshape(B, S // tq, vt.shape[1], tq), tq, tq)
        y_rg = _retention(seq3(rq), seq3(rk), seq3(rv), seq3(rg), row2(g_ret[l]), tr)
        h = _mix_cross(h, o_a.reshape(T, -1), y_rg.reshape(T, -1), gates, w_proj_mla[l].astype(BF16),
                       w_proj_ret[l].astype(BF16), w_out[l].astype(BF16), row2(g_cross[l]),
                       w_xq[l].astype(BF16), mkv, w_xo[l].astype(BF16), tm, S)
        last = l == depth - 1
        assert last, "only a single layer stack is supported"
        h = _ffn_out(h, row2(g_ffn[l]), w_ua, w_ub, w_ca, w_cb, b_ca, b_cb, w_d, row2(g_final), tm, S)
    return h.reshape(B, S, D)
```

```python
import functools

import numpy as np
import jax
import jax.numpy as jnp
from jax import lax
from jax.experimental import pallas as pl
from jax.experimental.pallas import tpu as pltpu

MLA_HEADS = 8
MLA_NOPE = 64
MLA_ROPE = 32
MLA_V = 64
MLA_Q_RANK = 256
MLA_KV_RANK = 128
RET_HEADS = 4
RET_DK = 128
RET_DV = 128
RET_CHUNK = 128
X_HEADS = 4
CONV_W = 3
ROPE_THETA = 10000.0
EPS = 1e-6

LANES = 128
SUBLANES = 8
HEAD_PAD = 128
V_ROWS = MLA_V + 16
FF_CHUNK = 256
VMEM_LIMIT = 56 * 1024 * 1024

BF16 = jnp.bfloat16
F32 = jnp.float32


def _rms(x, g):
    return x * lax.rsqrt(jnp.mean(x * x, axis=-1, keepdims=True) + EPS) * g


def _dot(a, b):
    return jnp.dot(a, b, preferred_element_type=F32)


def _dot_nt(a, b):
    return lax.dot_general(a, b, (((1,), (1,)), ((), ())), preferred_element_type=F32)


def _dot_tn(a, b):
    return lax.dot_general(a, b, (((0,), (0,)), ((), ())), preferred_element_type=F32)


def _const_spec(shape):
    nd = len(shape)
    return pl.BlockSpec(shape, lambda *_: (0,) * nd, pipeline_mode=pl.Buffered(1))


def _params(sem):
    return pltpu.CompilerParams(dimension_semantics=sem, vmem_limit_bytes=VMEM_LIMIT)


def _mem_kv_kernel(mem_ref, g_ref, w_ref, o_ref):
    n = _rms(mem_ref[...], g_ref[...]).astype(BF16)
    o_ref[...] = _dot(n, w_ref[...]).astype(o_ref.dtype)


def _mem_kv(mem, g_mem, w_xkv):
    B, M, D = mem.shape
    N = w_xkv.shape[1]
    return pl.pallas_call(
        _mem_kv_kernel,
        grid=(B,),
        in_specs=[pl.BlockSpec((None, M, D), lambda b: (b, 0, 0)),
                  _const_spec((1, D)), _const_spec((D, N))],
        out_specs=pl.BlockSpec((None, M, N), lambda b: (b, 0, 0)),
        out_shape=jax.ShapeDtypeStruct((B, M, N), BF16),
        compiler_params=_params(("parallel",)),
        name="mem_kv",
    )(mem, g_mem, w_xkv)


def _in_proj_kernel(x_ref, pos_ref, gmix_ref, wlat_ref, wret_ref, wgate_ref, bgate_ref,
                    gq_ref, wuq_ref, gkv_ref, wuk_ref, wuv_ref, vones_ref, inva_ref, invr_ref,
                    q_ref, k_ref, vt_ref, rq_ref, rk_ref, rv_ref, rg_ref, gate_ref):
    u = _rms(x_ref[...], gmix_ref[...]).astype(BF16)
    pos = pos_ref[...].astype(F32)

    ang_a = pos * inva_ref[...]
    cos_a, sin_a = jnp.cos(ang_a), jnp.sin(ang_a)
    lane = lax.broadcasted_iota(jnp.int32, ang_a.shape, 1)
    first_half = jnp.logical_and(lane >= MLA_NOPE, lane < MLA_NOPE + MLA_ROPE // 2)

    def rot_a(blk):
        partner = jnp.where(first_half,
                            pltpu.roll(blk, HEAD_PAD - MLA_ROPE // 2, 1),
                            pltpu.roll(blk, MLA_ROPE // 2, 1))
        return blk * cos_a + partner * sin_a

    lat = _dot(u, wlat_ref[...])
    cq = _rms(lat[:, :MLA_Q_RANK], gq_ref[...]).astype(BF16)
    ckv = _rms(lat[:, MLA_Q_RANK:MLA_Q_RANK + MLA_KV_RANK], gkv_ref[...]).astype(BF16)
    kr = rot_a(lat[:, MLA_Q_RANK + MLA_KV_RANK:])

    qf = _dot(cq, wuq_ref[...])
    kf = _dot(ckv, wuk_ref[...])
    q_scale = (MLA_NOPE + MLA_ROPE) ** -0.5 * np.log2(np.e)
    for h in range(MLA_HEADS):
        sl = slice(h * HEAD_PAD, (h + 1) * HEAD_PAD)
        q_ref[:, sl] = (rot_a(qf[:, sl]) * q_scale).astype(q_ref.dtype)
        k_ref[:, sl] = (kf[:, sl] + kr).astype(k_ref.dtype)
    vt_ref[...] = (_dot_nt(wuv_ref[...], ckv) + vones_ref[...]).astype(vt_ref.dtype)

    ang_r = pos * invr_ref[...]
    cos_r, sin_r = jnp.cos(ang_r), jnp.sin(ang_r)
    ret = _dot(u, wret_ref[...])
    hw = RET_HEADS * RET_DK
    k_scale = RET_DK ** -0.5
    for h in range(RET_HEADS):
        sl = slice(h * RET_DK, (h + 1) * RET_DK)
        bq = ret[:, sl]
        rq_ref[:, sl] = (bq * cos_r + pltpu.roll(bq, RET_DK // 2, 1) * sin_r).astype(rq_ref.dtype)
        bk = ret[:, hw + h * RET_DK: hw + (h + 1) * RET_DK]
        rk_ref[:, sl] = ((bk * cos_r + pltpu.roll(bk, RET_DK // 2, 1) * sin_r) * k_scale).astype(rk_ref.dtype)
    rv_ref[...] = ret[:, 2 * hw:3 * hw].astype(rv_ref.dtype)
    rg = ret[:, 3 * hw:]
    rg_ref[...] = (rg * jax.nn.sigmoid(rg)).astype(rg_ref.dtype)

    gate_ref[...] = jax.nn.sigmoid(_dot(u, wgate_ref[...]) + bgate_ref[...]).astype(gate_ref.dtype)


def _in_proj(x2, pos2, g_mix, w_lat, w_ret, w_gate, b_gate, g_q, w_uq, g_kv, w_uk, w_uv, v_ones,
             inv_a, inv_r, tm, tkv):
    T, D = x2.shape
    hw = RET_HEADS * RET_DK
    row = lambda n: pl.BlockSpec((tm, n), lambda i: (i, 0))
    consts = [g_mix, w_lat, w_ret, w_gate, b_gate, g_q, w_uq, g_kv, w_uk, w_uv, v_ones, inv_a, inv_r]
    out_widths = [MLA_HEADS * HEAD_PAD, MLA_HEADS * HEAD_PAD, hw, hw, hw, hw, 2 * D]
    out_dtypes = [BF16, BF16, F32, F32, BF16, BF16, BF16]
    vw = MLA_HEADS * V_ROWS
    per = tkv // tm
    vt_spec = pl.BlockSpec((None, vw, tm), lambda i: (i // per, 0, i % per))
    vt_shape = jax.ShapeDtypeStruct((T // tkv, vw, tkv), BF16)
    out_specs = [row(n) for n in out_widths]
    out_shape = [jax.ShapeDtypeStruct((T, n), dt) for n, dt in zip(out_widths, out_dtypes)]
    return pl.pallas_call(
        _in_proj_kernel,
        grid=(T // tm,),
        in_specs=[row(D), row(1)] + [_const_spec(c.shape) for c in consts],
        out_specs=out_specs[:2] + [vt_spec] + out_specs[2:],
        out_shape=out_shape[:2] + [vt_shape] + out_shape[2:],
        compiler_params=_params(("parallel",)),
        name="in_proj",
    )(x2, pos2, *consts)


def _mla_attn_kernel(q_ref, k_ref, vt_ref, o_ref, s_ref, mblk_ref, m_ref, acc_ref, *, tq):
    i = pl.program_id(2)
    heads = q_ref.shape[1] // HEAD_PAD

    m_ref[...] = jnp.full(m_ref.shape, -jnp.inf, F32)
    acc_ref[...] = jnp.zeros(acc_ref.shape, F32)

    def scores(j, slot):
        k0 = pl.multiple_of(j * tq, tq)
        for h in range(heads):
            sl = slice(h * HEAD_PAD, (h + 1) * HEAD_PAD)
            s = _dot_nt(k_ref[pl.ds(k0, tq), sl], q_ref[:, sl])
            s_ref[slot, h] = s
            mblk_ref[slot, h] = jnp.max(s, axis=0, keepdims=True)

    def consume(j, slot, diagonal):
        for h in range(heads):
            s = s_ref[slot, h]
            if diagonal:
                key = lax.broadcasted_iota(jnp.int32, s.shape, 0)
                qry = lax.broadcasted_iota(jnp.int32, s.shape, 1)
                s = jnp.where(key <= qry, s, -jnp.inf)
                m_blk = jnp.max(s, axis=0, keepdims=True)
            else:
                m_blk = mblk_ref[slot, h]
            m_old = m_ref[h]
            m_new = jnp.maximum(m_old, m_blk)
            alpha = jnp.exp2(m_old - m_new)
            p = jnp.exp2(s - m_new).astype(BF16)
            vt = vt_ref[j, h * V_ROWS:(h + 1) * V_ROWS, :]
            acc_ref[h] = alpha * acc_ref[h] + _dot(vt, p)
            m_ref[h] = m_new

    scores(0, 0)

    def body(t, carry):
        j = 2 * t
        scores(j + 1, 1)
        consume(j, 0, False)
        scores(j + 2, 0)
        consume(j + 1, 1, False)
        return carry

    lax.fori_loop(0, i // 2, body, 0)

    @pl.when(i % 2 == 0)
    def _():
        consume(i, 0, True)

    @pl.when(i % 2 == 1)
    def _():
        scores(i, 1)
        consume(i - 1, 0, False)
        consume(i, 1, True)

    out_t = jnp.concatenate([acc_ref[h, :MLA_V] / acc_ref[h, MLA_V:MLA_V + 1] for h in range(heads)], axis=0)
    o_ref[...] = out_t.T.astype(o_ref.dtype)


def _mla_attn(q, k, vt, tq, heads_per_step=2):
    B, S, _ = q.shape
    groups = MLA_HEADS // heads_per_step
    qw = heads_per_step * HEAD_PAD
    return pl.pallas_call(
        functools.partial(_mla_attn_kernel, tq=tq),
        grid=(B, groups, S // tq),
        in_specs=[pl.BlockSpec((None, tq, qw), lambda b, g, i: (b, i, g)),
                  pl.BlockSpec((None, S, qw), lambda b, g, i: (b, 0, g)),
                  pl.BlockSpec((None, S // tq, heads_per_step * V_ROWS, tq), lambda b, g, i: (b, 0, g, 0))],
        out_specs=pl.BlockSpec((None, tq, heads_per_step * MLA_V), lambda b, g, i: (b, i, g)),
        out_shape=jax.ShapeDtypeStruct((B, S, MLA_HEADS * MLA_V), BF16),
        scratch_shapes=[pltpu.VMEM((2, heads_per_step, tq, tq), F32),
                        pltpu.VMEM((2, heads_per_step, 1, tq), F32),
                        pltpu.VMEM((heads_per_step, 1, tq), F32),
                        pltpu.VMEM((heads_per_step, V_ROWS, tq), F32)],
        compiler_params=_params(("parallel", "parallel", "parallel")),
        name="mla_attn",
    )(q, k, vt)


def _retention_tables():
    C = RET_CHUNK
    log_g = np.log1p(-np.exp2(-5.0 - np.arange(RET_HEADS, dtype=np.float64)))
    idx = np.arange(C, dtype=np.float64)
    rel = idx[:, None] - idx[None, :]
    dmask = np.where(rel >= 0, np.exp(log_g[:, None, None] * np.maximum(rel, 0.0)), 0.0)
    zeta = np.exp(log_g[:, None] * (C - 1.0 - idx)[None, :])
    xi = np.exp(log_g[:, None] * (idx + 1.0)[None, :])
    decay = np.exp(log_g * C)
    f = lambda a: jnp.asarray(a, F32)
    return f(dmask), f(zeta[:, :, None]), f(xi[:, :, None]), f(np.broadcast_to(decay[:, None, None], (RET_HEADS, 1, LANES)))


def _retention_kernel(q_ref, k_ref, v_ref, rg_ref, gret_ref, dmask_ref, zeta_ref, xi_ref, decay_ref,
                      o_ref, state_ref, *, chunks):
    @pl.when(pl.program_id(1) == 0)
    def _():
        state_ref[...] = jnp.zeros(state_ref.shape, F32)

    C = RET_CHUNK
    for h in range(RET_HEADS):
        sl = slice(h * RET_DK, (h + 1) * RET_DK)
        dmask, zeta, xi = dmask_ref[h], zeta_ref[h], xi_ref[h]
        decay = decay_ref[h][:, :RET_DV]
        for c in range(chunks):
            rows = slice(c * C, (c + 1) * C)
            q = q_ref[rows, sl]
            k = k_ref[rows, sl]
            v = v_ref[rows, sl]
            s = (_dot_nt(q.astype(BF16), k.astype(BF16)) * dmask).astype(BF16)
            state = state_ref[h]
            y = _dot(s, v) + _dot((q * xi).astype(BF16), state.astype(BF16))
            state_ref[h] = state * decay + _dot_tn((k * zeta).astype(BF16), v)
            mu = jnp.mean(y, axis=-1, keepdims=True)
            yc = y - mu
            var = jnp.mean(yc * yc, axis=-1, keepdims=True)
            yn = yc * lax.rsqrt(var + EPS) * gret_ref[:, sl]
            o_ref[rows, sl] = (rg_ref[rows, sl].astype(F32) * yn).astype(o_ref.dtype)


def _retention(rq, rk, rv, rg, g_ret, tr):
    B, S, W = rq.shape
    tables = _retention_tables()
    blk = pl.BlockSpec((None, tr, W), lambda b, i: (b, i, 0))
    return pl.pallas_call(
        functools.partial(_retention_kernel, chunks=tr // RET_CHUNK),
        grid=(B, S // tr),
        in_specs=[blk, blk, blk, blk, _const_spec(g_ret.shape)] + [_const_spec(t.shape) for t in tables],
        out_specs=blk,
        out_shape=jax.ShapeDtypeStruct((B, S, W), BF16),
        scratch_shapes=[pltpu.VMEM((RET_HEADS, RET_DK, RET_DV), F32)],
        compiler_params=_params(("parallel", "arbitrary")),
        name="retention",
    )(rq, rk, rv, rg, g_ret, *tables)


def _mix_cross_kernel(x_ref, oa_ref, yr_ref, gate_ref, wpa_ref, wpr_ref, wout_ref, gc_ref, wxq_ref,
                      mkv_ref, wxo_ref, h_ref):
    D = x_ref.shape[1]
    hd = D // X_HEADS
    y_a = _dot(oa_ref[...], wpa_ref[...])
    y_r = _dot(yr_ref[...], wpr_ref[...])
    merged = gate_ref[:, :D].astype(F32) * y_a + gate_ref[:, D:].astype(F32) * y_r
    h1 = x_ref[...] + _dot(merged.astype(BF16), wout_ref[...])

    xq = _dot(_rms(h1, gc_ref[...]).astype(BF16), wxq_ref[...]).astype(BF16)
    scale = hd ** -0.5
    outs = []
    for h in range(X_HEADS):
        sl = slice(h * hd, (h + 1) * hd)
        s = _dot_nt(xq[:, sl], mkv_ref[:, sl]) * scale
        e = jnp.exp(s - jnp.max(s, axis=-1, keepdims=True))
        o = _dot(e.astype(BF16), mkv_ref[:, D + h * hd: D + (h + 1) * hd])
        outs.append((o / jnp.sum(e, axis=-1, keepdims=True)).astype(BF16))
    xo = jnp.concatenate(outs, axis=-1)
    h_ref[...] = h1 + _dot(xo, wxo_ref[...])


def _mix_cross(x2, o_a, y_rg, gates, w_pa, w_pr, w_out, g_cross, w_xq, mkv, w_xo, tm, seq):
    T, D = x2.shape
    M = mkv.shape[1]
    row = lambda n: pl.BlockSpec((tm, n), lambda i: (i, 0))
    steps_per_batch = seq // tm
    return pl.pallas_call(
        _mix_cross_kernel,
        grid=(T // tm,),
        in_specs=[row(D), row(o_a.shape[1]), row(y_rg.shape[1]), row(gates.shape[1]),
                  _const_spec(w_pa.shape), _const_spec(w_pr.shape), _const_spec(w_out.shape),
                  _const_spec(g_cross.shape), _const_spec(w_xq.shape),
                  pl.BlockSpec((None, M, 2 * D), lambda i: (i // steps_per_batch, 0, 0)),
                  _const_spec(w_xo.shape)],
        out_specs=row(D),
        out_shape=jax.ShapeDtypeStruct((T, D), F32),
        compiler_params=_params(("parallel",)),
        name="mix_cross",
    )(x2, o_a, y_rg, gates, w_pa, w_pr, w_out, g_cross, w_xq, mkv, w_xo)


def _ffn_kernel(h_ref, gf_ref, wua_ref, wub_ref, wca_ref, wcb_ref, bca_ref, bcb_ref, wd_ref, gfin_ref,
                o_ref, n_ref, act_ref, bufa_ref, bufb_ref, haloa_ref, halob_ref, *, steps_per_batch):
    tm = h_ref.shape[0]
    nch, _, fc = wua_ref.shape
    halo = SUBLANES

    @pl.when(pl.program_id(0) % steps_per_batch == 0)
    def _():
        haloa_ref[...] = jnp.zeros(haloa_ref.shape, F32)
        halob_ref[...] = jnp.zeros(halob_ref.shape, F32)

    n_ref[...] = _rms(h_ref[...], gf_ref[...]).astype(BF16)

    def conv(buf_ref, halo_ref, w_ref, b_ref, up, j):
        buf_ref[:halo, :] = halo_ref[j]
        buf_ref[halo:, :] = up
        halo_ref[j] = up[tm - halo:, :]
        w = w_ref[j]
        out = up * w[CONV_W - 1:CONV_W, :] + b_ref[j]
        for t in range(1, CONV_W):
            out = out + buf_ref[halo - t: halo - t + tm, :] * w[CONV_W - 1 - t:CONV_W - t, :]
        return out

    n = n_ref[...]
    ups = (_dot(n, wua_ref[0]), _dot(n, wub_ref[0]))
    for j in range(nch):
        ua, ub = ups
        if j + 1 < nch:
            ups = (_dot(n, wua_ref[j + 1]), _dot(n, wub_ref[j + 1]))
        a = conv(bufa_ref.at[j % 2], haloa_ref, wca_ref, bca_ref, ua, j)
        b = conv(bufb_ref.at[j % 2], halob_ref, wcb_ref, bcb_ref, ub, j)
        act_ref[:, j * fc:(j + 1) * fc] = (a * jax.nn.sigmoid(a) * b).astype(BF16)

    o_ref[...] = _rms(h_ref[...] + _dot(act_ref[...], wd_ref[...]), gfin_ref[...])


def _ffn_out(h2, g_ffn, w_ua, w_ub, w_ca, w_cb, b_ca, b_cb, w_d, g_final, tm, seq):
    T, D = h2.shape
    nch, _, fc = w_ua.shape
    row = pl.BlockSpec((tm, D), lambda i: (i, 0))
    consts = [g_ffn, w_ua, w_ub, w_ca, w_cb, b_ca, b_cb, w_d, g_final]
    return pl.pallas_call(
        functools.partial(_ffn_kernel, steps_per_batch=seq // tm),
        grid=(T // tm,),
        in_specs=[row] + [_const_spec(c.shape) for c in consts],
        out_specs=row,
        out_shape=jax.ShapeDtypeStruct((T, D), F32),
        scratch_shapes=[pltpu.VMEM((tm, D), BF16), pltpu.VMEM((tm, nch * fc), BF16),
                        pltpu.VMEM((2, tm + SUBLANES, fc), F32), pltpu.VMEM((2, tm + SUBLANES, fc), F32),
                        pltpu.VMEM((nch, SUBLANES, fc), F32), pltpu.VMEM((nch, SUBLANES, fc), F32)],
        compiler_params=_params(("arbitrary",)),
        name="ffn_out",
    )(h2, *consts)


def _rope_inv(half):
    return ROPE_THETA ** (-np.arange(half, dtype=np.float32) / np.float32(half))


def _mla_column_maps():
    half = MLA_ROPE // 2
    used = MLA_NOPE + MLA_ROPE
    q_cols = np.zeros((MLA_HEADS, HEAD_PAD), np.int32)
    k_cols = np.zeros((MLA_HEADS, HEAD_PAD), np.int32)
    valid_q = np.zeros((MLA_HEADS, HEAD_PAD), bool)
    valid_k = np.zeros((MLA_HEADS, HEAD_PAD), bool)
    for h in range(MLA_HEADS):
        q_cols[h, :used] = h * used + np.arange(used)
        valid_q[h, :used] = True
        k_cols[h, :MLA_NOPE] = h * (MLA_NOPE + MLA_V) + np.arange(MLA_NOPE)
        valid_k[h, :MLA_NOPE] = True
    v_cols = np.zeros((MLA_HEADS, V_ROWS), np.int32)
    valid_v = np.zeros((MLA_HEADS, V_ROWS), bool)
    for h in range(MLA_HEADS):
        v_cols[h, :MLA_V] = h * (MLA_NOPE + MLA_V) + MLA_NOPE + np.arange(MLA_V)
        valid_v[h, :MLA_V] = True
    del half
    return (q_cols.reshape(-1), valid_q.reshape(-1), k_cols.reshape(-1), valid_k.reshape(-1),
            v_cols.reshape(-1), valid_v.reshape(-1))


def kernel(x, mem, positions, g_mix, w_in, b_gate, g_q_lat, w_uq, g_kv_lat, w_ukv, w_proj_mla, g_ret,
           w_proj_ret, w_out, g_cross, g_mem, w_xq, w_xkv, w_xo, g_ffn, w_up, w_conv, b_conv, w_down, g_final):
    B, S, D = x.shape
    T = B * S
    depth = w_in.shape[0]
    d_ff = w_down.shape[1]
    hw = RET_HEADS * RET_DK
    tm = min(256, S)
    tq = min(512, S)
    tr = min(512, S)
    tf = min(512, S)

    half_a = MLA_ROPE // 2
    inv_a = np.zeros((1, HEAD_PAD), np.float32)
    inv_a[0, MLA_NOPE:MLA_NOPE + half_a] = -_rope_inv(half_a)
    inv_a[0, MLA_NOPE + half_a:MLA_NOPE + MLA_ROPE] = _rope_inv(half_a)
    inv_r = np.concatenate([-_rope_inv(RET_DK // 2), _rope_inv(RET_DK // 2)])[None, :]
    q_cols, valid_q, k_cols, valid_k, v_cols, valid_v = _mla_column_maps()
    v_ones = jnp.asarray(np.where(valid_v, 0.0, 1.0)[:, None], F32)

    h = x.reshape(T, D)
    pos2 = positions.reshape(T, 1)
    row2 = lambda a: a.reshape(1, -1)
    for l in range(depth):
        c0 = MLA_Q_RANK + MLA_KV_RANK
        w_kr = w_in[l][:, c0:c0 + MLA_ROPE]
        kr_pad = jnp.zeros((D, HEAD_PAD), F32).at[:, MLA_NOPE:MLA_NOPE + MLA_ROPE].set(w_kr)
        w_lat = jnp.concatenate([w_in[l][:, :c0], kr_pad], axis=1).astype(BF16)
        c1 = c0 + MLA_ROPE
        w_ret = w_in[l][:, c1:c1 + 4 * hw].astype(BF16)
        w_gate = w_in[l][:, c1 + 4 * hw:].astype(BF16)
        w_uq_p = jnp.where(valid_q[None, :], w_uq[l][:, q_cols], 0.0).astype(BF16)
        w_uk_p = jnp.where(valid_k[None, :], w_ukv[l][:, k_cols], 0.0).astype(BF16)
        w_uv = jnp.where(valid_v[None, :], w_ukv[l][:, v_cols], 0.0).T.astype(BF16)
        nch = d_ff // FF_CHUNK
        chunked = lambda w: w.reshape(w.shape[0], nch, FF_CHUNK).transpose(1, 0, 2)
        w_ua = chunked(w_up[l][:, :d_ff]).astype(BF16)
        w_ub = chunked(w_up[l][:, d_ff:]).astype(BF16)
        w_ca, w_cb = chunked(w_conv[l][:, :d_ff]), chunked(w_conv[l][:, d_ff:])
        b_ca, b_cb = chunked(b_conv[l][None, :d_ff]), chunked(b_conv[l][None, d_ff:])
        w_d = w_down[l].astype(BF16)

        mkv = _mem_kv(mem, row2(g_mem[l]), w_xkv[l].astype(BF16))
        q, k, vt, rq, rk, rv, rg, gates = _in_proj(
            h, pos2, row2(g_mix[l]), w_lat, w_ret, w_gate, row2(b_gate[l]), row2(g_q_lat[l]), w_uq_p,
            row2(g_kv_lat[l]), w_uk_p, w_uv, v_ones, jnp.asarray(inv_a), jnp.asarray(inv_r), tm, tq)
        seq3 = lambda a: a.reshape(B, S, a.shape[-1])
        o_a = _mla_attn(seq3(q), seq3(k), vt.reshape(B, S // tq, vt.shape[1], tq), tq)
        y_rg = _retention(seq3(rq), seq3(rk), seq3(rv), seq3(rg), row2(g_ret[l]), tr)
        h = _mix_cross(h, o_a.reshape(T, -1), y_rg.reshape(T, -1), gates, w_proj_mla[l].astype(BF16),
                       w_proj_ret[l].astype(BF16), w_out[l].astype(BF16), row2(g_cross[l]),
                       w_xq[l].astype(BF16), mkv, w_xo[l].astype(BF16), tm, S)
        last = l == depth - 1
        assert last, "only a single layer stack is supported"
        h = _ffn_out(h, row2(g_ffn[l]), w_ua, w_ub, w_ca, w_cb, b_ca, b_cb, w_d, row2(g_final), tf, S)
    return h.reshape(B, S, D)
```

```python
import functools

import numpy as np
import jax
import jax.numpy as jnp
from jax import lax
from jax.experimental import pallas as pl
from jax.experimental.pallas import tpu as pltpu

MLA_HEADS = 8
MLA_NOPE = 64
MLA_ROPE = 32
MLA_V = 64
MLA_Q_RANK = 256
MLA_KV_RANK = 128
RET_HEADS = 4
RET_DK = 128
RET_DV = 128
RET_CHUNK = 128
X_HEADS = 4
CONV_W = 3
ROPE_THETA = 10000.0
EPS = 1e-6

LANES = 128
SUBLANES = 8
HEAD_PAD = 128
V_ROWS = MLA_V + 16
FF_CHUNK = 256
VMEM_LIMIT = 56 * 1024 * 1024

BF16 = jnp.bfloat16
F32 = jnp.float32


def _rms(x, g):
    return x * lax.rsqrt(jnp.mean(x * x, axis=-1, keepdims=True) + EPS) * g


def _dot(a, b):
    return jnp.dot(a, b, preferred_element_type=F32)


def _dot_nt(a, b):
    return lax.dot_general(a, b, (((1,), (1,)), ((), ())), preferred_element_type=F32)


def _dot_tn(a, b):
    return lax.dot_general(a, b, (((0,), (0,)), ((), ())), preferred_element_type=F32)


def _const_spec(shape):
    nd = len(shape)
    return pl.BlockSpec(shape, lambda *_: (0,) * nd, pipeline_mode=pl.Buffered(1))


def _params(sem):
    return pltpu.CompilerParams(dimension_semantics=sem, vmem_limit_bytes=VMEM_LIMIT)


def _mem_kv_kernel(mem_ref, g_ref, w_ref, o_ref):
    n = _rms(mem_ref[...], g_ref[...]).astype(BF16)
    o_ref[...] = _dot(n, w_ref[...]).astype(o_ref.dtype)


def _mem_kv(mem, g_mem, w_xkv):
    B, M, D = mem.shape
    N = w_xkv.shape[1]
    return pl.pallas_call(
        _mem_kv_kernel,
        grid=(B,),
        in_specs=[pl.BlockSpec((None, M, D), lambda b: (b, 0, 0)),
                  _const_spec((1, D)), _const_spec((D, N))],
        out_specs=pl.BlockSpec((None, M, N), lambda b: (b, 0, 0)),
        out_shape=jax.ShapeDtypeStruct((B, M, N), BF16),
        compiler_params=_params(("parallel",)),
        name="mem_kv",
    )(mem, g_mem, w_xkv)


def _in_proj_kernel(x_ref, pos_ref, gmix_ref, wlat_ref, wret_ref, wgate_ref, bgate_ref,
                    gq_ref, wuq_ref, gkv_ref, wuk_ref, wuv_ref, vones_ref, inv_ref,
                    q_ref, k_ref, vt_ref, rq_ref, rk_ref, rv_ref, rg_ref, gate_ref):
    u = _rms(x_ref[...], gmix_ref[...]).astype(BF16)
    lat = _dot(u, wlat_ref[...])
    ret = _dot(u, wret_ref[...])
    gate_pre = _dot(u, wgate_ref[...])

    ang = pos_ref[...].astype(F32) * inv_ref[...]
    cos_p, sin_p = jnp.cos(ang), jnp.sin(ang)
    lane = lax.broadcasted_iota(jnp.int32, ang.shape, 1)
    half_r = RET_DK // 2
    half_a = MLA_ROPE // 2
    cos_r = jnp.where(lane < half_r, cos_p, pltpu.roll(cos_p, half_r, 1))
    sin_r = jnp.where(lane < half_r, -sin_p, pltpu.roll(sin_p, half_r, 1))
    x1_a = jnp.logical_and(lane >= MLA_NOPE, lane < MLA_NOPE + half_a)
    x2_a = jnp.logical_and(lane >= MLA_NOPE + half_a, lane < MLA_NOPE + MLA_ROPE)
    cos_a = jnp.where(x1_a, cos_p, jnp.where(x2_a, pltpu.roll(cos_p, half_a, 1), 1.0))
    sin_a = jnp.where(x1_a, -sin_p, jnp.where(x2_a, pltpu.roll(sin_p, half_a, 1), 0.0))

    def rot_a(blk):
        partner = jnp.where(x1_a,
                            pltpu.roll(blk, HEAD_PAD - half_a, 1),
                            pltpu.roll(blk, half_a, 1))
        return blk * cos_a + partner * sin_a

    cq = _rms(lat[:, :MLA_Q_RANK], gq_ref[...]).astype(BF16)
    ckv = _rms(lat[:, MLA_Q_RANK:MLA_Q_RANK + MLA_KV_RANK], gkv_ref[...]).astype(BF16)
    kr = rot_a(lat[:, MLA_Q_RANK + MLA_KV_RANK:])

    qf = _dot(cq, wuq_ref[...])
    kf = _dot(ckv, wuk_ref[...])
    q_scale = (MLA_NOPE + MLA_ROPE) ** -0.5 * np.log2(np.e)
    for h in range(MLA_HEADS):
        sl = slice(h * HEAD_PAD, (h + 1) * HEAD_PAD)
        q_ref[:, sl] = (rot_a(qf[:, sl]) * q_scale).astype(q_ref.dtype)
        k_ref[:, sl] = (kf[:, sl] + kr).astype(k_ref.dtype)
    vt_ref[...] = (_dot_nt(wuv_ref[...], ckv) + vones_ref[...]).astype(vt_ref.dtype)

    hw = RET_HEADS * RET_DK
    k_scale = RET_DK ** -0.5
    for h in range(RET_HEADS):
        sl = slice(h * RET_DK, (h + 1) * RET_DK)
        bq = ret[:, sl]
        rq_ref[:, sl] = (bq * cos_r + pltpu.roll(bq, RET_DK // 2, 1) * sin_r).astype(rq_ref.dtype)
        bk = ret[:, hw + h * RET_DK: hw + (h + 1) * RET_DK]
        rk_ref[:, sl] = ((bk * cos_r + pltpu.roll(bk, RET_DK // 2, 1) * sin_r) * k_scale).astype(rk_ref.dtype)
    rv_ref[...] = ret[:, 2 * hw:3 * hw].astype(rv_ref.dtype)
    rg = ret[:, 3 * hw:]
    rg_ref[...] = (rg * jax.nn.sigmoid(rg)).astype(rg_ref.dtype)

    gate_ref[...] = jax.nn.sigmoid(gate_pre + bgate_ref[...]).astype(gate_ref.dtype)


def _in_proj(x2, pos2, g_mix, w_lat, w_ret, w_gate, b_gate, g_q, w_uq, g_kv, w_uk, w_uv, v_ones,
             inv_pack, tm, tkv):
    T, D = x2.shape
    hw = RET_HEADS * RET_DK
    row = lambda n: pl.BlockSpec((tm, n), lambda i: (i, 0))
    consts = [g_mix, w_lat, w_ret, w_gate, b_gate, g_q, w_uq, g_kv, w_uk, w_uv, v_ones, inv_pack]
    out_widths = [MLA_HEADS * HEAD_PAD, MLA_HEADS * HEAD_PAD, hw, hw, hw, hw, 2 * D]
    out_dtypes = [BF16, BF16, F32, F32, BF16, BF16, BF16]
    vw = MLA_HEADS * V_ROWS
    per = tkv // tm
    vt_spec = pl.BlockSpec((None, vw, tm), lambda i: (i // per, 0, i % per))
    vt_shape = jax.ShapeDtypeStruct((T // tkv, vw, tkv), BF16)
    out_specs = [row(n) for n in out_widths]
    out_shape = [jax.ShapeDtypeStruct((T, n), dt) for n, dt in zip(out_widths, out_dtypes)]
    return pl.pallas_call(
        _in_proj_kernel,
        grid=(T // tm,),
        in_specs=[row(D), row(1)] + [_const_spec(c.shape) for c in consts],
        out_specs=out_specs[:2] + [vt_spec] + out_specs[2:],
        out_shape=out_shape[:2] + [vt_shape] + out_shape[2:],
        compiler_params=_params(("parallel",)),
        name="in_proj",
    )(x2, pos2, *consts)


def _mla_attn_kernel(q_ref, k_ref, vt_ref, o_ref, s_ref, mblk_ref, m_ref, acc_ref, *, tq):
    i = pl.program_id(2)
    heads = q_ref.shape[1] // HEAD_PAD

    m_ref[...] = jnp.full(m_ref.shape, -jnp.inf, F32)
    acc_ref[...] = jnp.zeros(acc_ref.shape, F32)

    def scores(j, slot):
        k0 = pl.multiple_of(j * tq, tq)
        for h in range(heads):
            sl = slice(h * HEAD_PAD, (h + 1) * HEAD_PAD)
            s = _dot_nt(k_ref[pl.ds(k0, tq), sl], q_ref[:, sl])
            s_ref[slot, h] = s
            mblk_ref[slot, h] = jnp.max(s, axis=0, keepdims=True)

    def consume(j, slot, diagonal):
        for h in range(heads):
            s = s_ref[slot, h]
            if diagonal:
                key = lax.broadcasted_iota(jnp.int32, s.shape, 0)
                qry = lax.broadcasted_iota(jnp.int32, s.shape, 1)
                s = jnp.where(key <= qry, s, -jnp.inf)
                m_blk = jnp.max(s, axis=0, keepdims=True)
            else:
                m_blk = mblk_ref[slot, h]
            m_old = m_ref[h]
            m_new = jnp.maximum(m_old, m_blk)
            alpha = jnp.exp2(m_old - m_new)
            p = jnp.exp2(s - m_new).astype(BF16)
            vt = vt_ref[j, h * V_ROWS:(h + 1) * V_ROWS, :]
            acc_ref[h] = alpha * acc_ref[h] + _dot(vt, p)
            m_ref[h] = m_new

    scores(0, 0)

    def body(t, carry):
        j = 2 * t
        scores(j + 1, 1)
        consume(j, 0, False)
        scores(j + 2, 0)
        consume(j + 1, 1, False)
        return carry

    lax.fori_loop(0, i // 2, body, 0)

    @pl.when(i % 2 == 0)
    def _():
        consume(i, 0, True)

    @pl.when(i % 2 == 1)
    def _():
        scores(i, 1)
        consume(i - 1, 0, False)
        consume(i, 1, True)

    out_t = jnp.concatenate([acc_ref[h, :MLA_V] / acc_ref[h, MLA_V:MLA_V + 1] for h in range(heads)], axis=0)
    o_ref[...] = out_t.T.astype(o_ref.dtype)


def _mla_attn(q, k, vt, tq, heads_per_step=2):
    B, S, _ = q.shape
    groups = MLA_HEADS // heads_per_step
    qw = heads_per_step * HEAD_PAD
    return pl.pallas_call(
        functools.partial(_mla_attn_kernel, tq=tq),
        grid=(B, groups, S // tq),
        in_specs=[pl.BlockSpec((None, tq, qw), lambda b, g, i: (b, i, g)),
                  pl.BlockSpec((None, S, qw), lambda b, g, i: (b, 0, g)),
                  pl.BlockSpec((None, S // tq, heads_per_step * V_ROWS, tq), lambda b, g, i: (b, 0, g, 0))],
        out_specs=pl.BlockSpec((None, tq, heads_per_step * MLA_V), lambda b, g, i: (b, i, g)),
        out_shape=jax.ShapeDtypeStruct((B, S, MLA_HEADS * MLA_V), BF16),
        scratch_shapes=[pltpu.VMEM((2, heads_per_step, tq, tq), F32),
                        pltpu.VMEM((2, heads_per_step, 1, tq), F32),
                        pltpu.VMEM((heads_per_step, 1, tq), F32),
                        pltpu.VMEM((heads_per_step, V_ROWS, tq), F32)],
        compiler_params=_params(("parallel", "parallel", "parallel")),
        name="mla_attn",
    )(q, k, vt)


def _retention_tables():
    C = RET_CHUNK
    log_g = np.log1p(-np.exp2(-5.0 - np.arange(RET_HEADS, dtype=np.float64)))
    idx = np.arange(C, dtype=np.float64)
    rel = idx[:, None] - idx[None, :]
    dmask = np.where(rel >= 0, np.exp(log_g[:, None, None] * np.maximum(rel, 0.0)), 0.0)
    zeta = np.exp(log_g[:, None] * (C - 1.0 - idx)[None, :])
    xi = np.exp(log_g[:, None] * (idx + 1.0)[None, :])
    decay = np.exp(log_g * C)
    f = lambda a: jnp.asarray(a, F32)
    return f(dmask), f(zeta[:, :, None]), f(xi[:, :, None]), f(np.broadcast_to(decay[:, None, None], (RET_HEADS, 1, LANES)))


def _retention_kernel(q_ref, k_ref, v_ref, rg_ref, gret_ref, dmask_ref, zeta_ref, xi_ref, decay_ref,
                      o_ref, state_ref, *, chunks):
    @pl.when(pl.program_id(1) == 0)
    def _():
        state_ref[...] = jnp.zeros(state_ref.shape, F32)

    C = RET_CHUNK
    for h in range(RET_HEADS):
        sl = slice(h * RET_DK, (h + 1) * RET_DK)
        dmask, zeta, xi = dmask_ref[h], zeta_ref[h], xi_ref[h]
        decay = decay_ref[h][:, :RET_DV]
        inner, kv, qx = [], [], []
        for c in range(chunks):
            rows = slice(c * C, (c + 1) * C)
            q, k, v = q_ref[rows, sl], k_ref[rows, sl], v_ref[rows, sl]
            s = (_dot_nt(q.astype(BF16), k.astype(BF16)) * dmask).astype(BF16)
            inner.append(_dot(s, v))
            kv.append(_dot_tn((k * zeta).astype(BF16), v))
            qx.append((q * xi).astype(BF16))
        state = state_ref[h]
        for c in range(chunks):
            rows = slice(c * C, (c + 1) * C)
            y = inner[c] + _dot(qx[c], state.astype(BF16))
            state = state * decay + kv[c]
            mu = jnp.mean(y, axis=-1, keepdims=True)
            yc = y - mu
            var = jnp.mean(yc * yc, axis=-1, keepdims=True)
            yn = yc * lax.rsqrt(var + EPS) * gret_ref[:, sl]
            o_ref[rows, sl] = (rg_ref[rows, sl].astype(F32) * yn).astype(o_ref.dtype)
        state_ref[h] = state


def _retention(rq, rk, rv, rg, g_ret, tr):
    B, S, W = rq.shape
    tables = _retention_tables()
    blk = pl.BlockSpec((None, tr, W), lambda b, i: (b, i, 0))
    return pl.pallas_call(
        functools.partial(_retention_kernel, chunks=tr // RET_CHUNK),
        grid=(B, S // tr),
        in_specs=[blk, blk, blk, blk, _const_spec(g_ret.shape)] + [_const_spec(t.shape) for t in tables],
        out_specs=blk,
        out_shape=jax.ShapeDtypeStruct((B, S, W), BF16),
        scratch_shapes=[pltpu.VMEM((RET_HEADS, RET_DK, RET_DV), F32)],
        compiler_params=_params(("parallel", "arbitrary")),
        name="retention",
    )(rq, rk, rv, rg, g_ret, *tables)


def _mix_cross_kernel(x_ref, oa_ref, yr_ref, gate_ref, wpa_ref, wpr_ref, wout_ref, gc_ref, wxq_ref,
                      mkv_ref, wxo_ref, h_ref):
    D = x_ref.shape[1]
    hd = D // X_HEADS
    y_a = _dot(oa_ref[...], wpa_ref[...])
    y_r = _dot(yr_ref[...], wpr_ref[...])
    merged = gate_ref[:, :D].astype(F32) * y_a + gate_ref[:, D:].astype(F32) * y_r
    h1 = x_ref[...] + _dot(merged.astype(BF16), wout_ref[...])

    xq = _dot(_rms(h1, gc_ref[...]).astype(BF16), wxq_ref[...]).astype(BF16)
    scale = hd ** -0.5
    outs = []
    for h in range(X_HEADS):
        sl = slice(h * hd, (h + 1) * hd)
        s = _dot_nt(xq[:, sl], mkv_ref[:, sl]) * scale
        e = jnp.exp(s - jnp.max(s, axis=-1, keepdims=True))
        o = _dot(e.astype(BF16), mkv_ref[:, D + h * hd: D + (h + 1) * hd])
        outs.append((o / jnp.sum(e, axis=-1, keepdims=True)).astype(BF16))
    xo = jnp.concatenate(outs, axis=-1)
    h_ref[...] = h1 + _dot(xo, wxo_ref[...])


def _mix_cross(x2, o_a, y_rg, gates, w_pa, w_pr, w_out, g_cross, w_xq, mkv, w_xo, tm, seq):
    T, D = x2.shape
    M = mkv.shape[1]
    row = lambda n: pl.BlockSpec((tm, n), lambda i: (i, 0))
    steps_per_batch = seq // tm
    return pl.pallas_call(
        _mix_cross_kernel,
        grid=(T // tm,),
        in_specs=[row(D), row(o_a.shape[1]), row(y_rg.shape[1]), row(gates.shape[1]),
                  _const_spec(w_pa.shape), _const_spec(w_pr.shape), _const_spec(w_out.shape),
                  _const_spec(g_cross.shape), _const_spec(w_xq.shape),
                  pl.BlockSpec((None, M, 2 * D), lambda i: (i // steps_per_batch, 0, 0)),
                  _const_spec(w_xo.shape)],
        out_specs=row(D),
        out_shape=jax.ShapeDtypeStruct((T, D), F32),
        compiler_params=_params(("parallel",)),
        name="mix_cross",
    )(x2, o_a, y_rg, gates, w_pa, w_pr, w_out, g_cross, w_xq, mkv, w_xo)


def _ffn_kernel(h_ref, gf_ref, wu_ref, wc_ref, bc_ref, wd_ref, gfin_ref,
                o_ref, n_ref, act_ref, bufa_ref, bufb_ref, haloa_ref, halob_ref, *, steps_per_batch):
    tm = h_ref.shape[0]
    nch, _, fc = haloa_ref.shape
    d_ff = nch * fc
    halo = SUBLANES

    @pl.when(pl.program_id(0) % steps_per_batch == 0)
    def _():
        haloa_ref[...] = jnp.zeros(haloa_ref.shape, F32)
        halob_ref[...] = jnp.zeros(halob_ref.shape, F32)

    n_ref[...] = _rms(h_ref[...], gf_ref[...]).astype(BF16)

    def conv(buf_ref, halo_ref, col, up, j):
        buf_ref[:halo, :] = halo_ref[j]
        buf_ref[halo:, :] = up
        halo_ref[j] = up[tm - halo:, :]
        w = wc_ref[:, col:col + fc]
        out = up * w[CONV_W - 1:CONV_W, :] + bc_ref[:, col:col + fc]
        for t in range(1, CONV_W):
            out = out + buf_ref[halo - t: halo - t + tm, :] * w[CONV_W - 1 - t:CONV_W - t, :]
        return out

    n = n_ref[...]
    up_pair = lambda j: (_dot(n, wu_ref[:, j * fc:(j + 1) * fc]), _dot(n, wu_ref[:, d_ff + j * fc:d_ff + (j + 1) * fc]))
    ups = up_pair(0)
    for j in range(nch):
        ua, ub = ups
        if j + 1 < nch:
            ups = up_pair(j + 1)
        a = conv(bufa_ref.at[j % 2], haloa_ref, j * fc, ua, j)
        b = conv(bufb_ref.at[j % 2], halob_ref, d_ff + j * fc, ub, j)
        act_ref[:, j * fc:(j + 1) * fc] = (a * jax.nn.sigmoid(a) * b).astype(BF16)

    o_ref[...] = _rms(h_ref[...] + _dot(act_ref[...], wd_ref[...]), gfin_ref[...])


def _ffn_out(h2, g_ffn, w_u, w_c, b_c, w_d, g_final, tm, seq):
    T, D = h2.shape
    fc = FF_CHUNK
    nch = w_d.shape[0] // fc
    row = pl.BlockSpec((tm, D), lambda i: (i, 0))
    consts = [g_ffn, w_u, w_c, b_c, w_d, g_final]
    return pl.pallas_call(
        functools.partial(_ffn_kernel, steps_per_batch=seq // tm),
        grid=(T // tm,),
        in_specs=[row] + [_const_spec(c.shape) for c in consts],
        out_specs=row,
        out_shape=jax.ShapeDtypeStruct((T, D), F32),
        scratch_shapes=[pltpu.VMEM((tm, D), BF16), pltpu.VMEM((tm, nch * fc), BF16),
                        pltpu.VMEM((2, tm + SUBLANES, fc), F32), pltpu.VMEM((2, tm + SUBLANES, fc), F32),
                        pltpu.VMEM((nch, SUBLANES, fc), F32), pltpu.VMEM((nch, SUBLANES, fc), F32)],
        compiler_params=_params(("arbitrary",)),
        name="ffn_out",
    )(h2, *consts)


def _rope_inv(half):
    return ROPE_THETA ** (-np.arange(half, dtype=np.float32) / np.float32(half))


def _mla_column_maps():
    half = MLA_ROPE // 2
    used = MLA_NOPE + MLA_ROPE
    q_cols = np.zeros((MLA_HEADS, HEAD_PAD), np.int32)
    k_cols = np.zeros((MLA_HEADS, HEAD_PAD), np.int32)
    valid_q = np.zeros((MLA_HEADS, HEAD_PAD), bool)
    valid_k = np.zeros((MLA_HEADS, HEAD_PAD), bool)
    for h in range(MLA_HEADS):
        q_cols[h, :used] = h * used + np.arange(used)
        valid_q[h, :used] = True
        k_cols[h, :MLA_NOPE] = h * (MLA_NOPE + MLA_V) + np.arange(MLA_NOPE)
        valid_k[h, :MLA_NOPE] = True
    v_cols = np.zeros((MLA_HEADS, V_ROWS), np.int32)
    valid_v = np.zeros((MLA_HEADS, V_ROWS), bool)
    for h in range(MLA_HEADS):
        v_cols[h, :MLA_V] = h * (MLA_NOPE + MLA_V) + MLA_NOPE + np.arange(MLA_V)
        valid_v[h, :MLA_V] = True
    del half
    return (q_cols.reshape(-1), valid_q.reshape(-1), k_cols.reshape(-1), valid_k.reshape(-1),
            v_cols.reshape(-1), valid_v.reshape(-1))


def kernel(x, mem, positions, g_mix, w_in, b_gate, g_q_lat, w_uq, g_kv_lat, w_ukv, w_proj_mla, g_ret,
           w_proj_ret, w_out, g_cross, g_mem, w_xq, w_xkv, w_xo, g_ffn, w_up, w_conv, b_conv, w_down, g_final):
    B, S, D = x.shape
    T = B * S
    depth = w_in.shape[0]
    d_ff = w_down.shape[1]
    hw = RET_HEADS * RET_DK
    tm = min(512, S)
    tq = min(512, S)
    tr = min(512, S)
    tf = min(512, S)

    half_a = MLA_ROPE // 2
    inv_pack = np.zeros((1, LANES), np.float32)
    inv_pack[0, :RET_DK // 2] = _rope_inv(RET_DK // 2)
    inv_pack[0, MLA_NOPE:MLA_NOPE + half_a] = _rope_inv(half_a)
    q_cols, valid_q, k_cols, valid_k, v_cols, valid_v = _mla_column_maps()
    v_ones = jnp.asarray(np.where(valid_v, 0.0, 1.0)[:, None], F32)

    h = x.reshape(T, D)
    pos2 = positions.reshape(T, 1)
    row2 = lambda a: a.reshape(1, -1)
    for l in range(depth):
        c0 = MLA_Q_RANK + MLA_KV_RANK
        w_kr = w_in[l][:, c0:c0 + MLA_ROPE]
        kr_pad = jnp.zeros((D, HEAD_PAD), F32).at[:, MLA_NOPE:MLA_NOPE + MLA_ROPE].set(w_kr)
        w_lat = jnp.concatenate([w_in[l][:, :c0], kr_pad], axis=1).astype(BF16)
        c1 = c0 + MLA_ROPE
        w_ret = w_in[l][:, c1:c1 + 4 * hw].astype(BF16)
        w_gate = w_in[l][:, c1 + 4 * hw:].astype(BF16)
        w_uq_p = jnp.where(valid_q[None, :], w_uq[l][:, q_cols], 0.0).astype(BF16)
        w_uk_p = jnp.where(valid_k[None, :], w_ukv[l][:, k_cols], 0.0).astype(BF16)
        w_uv = jnp.where(valid_v[None, :], w_ukv[l][:, v_cols], 0.0).T.astype(BF16)
        assert d_ff % FF_CHUNK == 0
        w_d = w_down[l].astype(BF16)

        mkv = _mem_kv(mem, row2(g_mem[l]), w_xkv[l].astype(BF16))
        q, k, vt, rq, rk, rv, rg, gates = _in_proj(
            h, pos2, row2(g_mix[l]), w_lat, w_ret, w_gate, row2(b_gate[l]), row2(g_q_lat[l]), w_uq_p,
            row2(g_kv_lat[l]), w_uk_p, w_uv, v_ones, jnp.asarray(inv_pack), tm, tq)
        seq3 = lambda a: a.reshape(B, S, a.shape[-1])
        o_a = _mla_attn(seq3(q), seq3(k), vt.reshape(B, S // tq, vt.shape[1], tq), tq)
        y_rg = _retention(seq3(rq), seq3(rk), seq3(rv), seq3(rg), row2(g_ret[l]), tr)
        h = _mix_cross(h, o_a.reshape(T, -1), y_rg.reshape(T, -1), gates, w_proj_mla[l].astype(BF16),
                       w_proj_ret[l].astype(BF16), w_out[l].astype(BF16), row2(g_cross[l]),
                       w_xq[l].astype(BF16), mkv, w_xo[l].astype(BF16), tm, S)
        last = l == depth - 1
        assert last, "only a single layer stack is supported"
        h = _ffn_out(h, row2(g_ffn[l]), w_up[l].astype(BF16), w_conv[l], row2(b_conv[l]), w_d, row2(g_final), tf, S)
    return h.reshape(B, S, D)
```

```python
import functools

import numpy as np
import jax
import jax.numpy as jnp
from jax import lax
from jax.experimental import pallas as pl
from jax.experimental.pallas import tpu as pltpu

MLA_HEADS = 8
MLA_NOPE = 64
MLA_ROPE = 32
MLA_V = 64
MLA_Q_RANK = 256
MLA_KV_RANK = 128
RET_HEADS = 4
RET_DK = 128
RET_DV = 128
RET_CHUNK = 128
X_HEADS = 4
CONV_W = 3
ROPE_THETA = 10000.0
EPS = 1e-6

LANES = 128
SUBLANES = 8
HEAD_PAD = 128
V_ROWS = MLA_V + 16
LOOP_BLOCKS = 4
FF_CHUNK = 256
VMEM_LIMIT = 56 * 1024 * 1024

BF16 = jnp.bfloat16
F32 = jnp.float32


def _rms(x, g):
    return x * lax.rsqrt(jnp.mean(x * x, axis=-1, keepdims=True) + EPS) * g


def _dot(a, b):
    return jnp.dot(a, b, preferred_element_type=F32)


def _dot_nt(a, b):
    return lax.dot_general(a, b, (((1,), (1,)), ((), ())), preferred_element_type=F32)


def _dot_tn(a, b):
    return lax.dot_general(a, b, (((0,), (0,)), ((), ())), preferred_element_type=F32)


def _const_spec(shape):
    nd = len(shape)
    return pl.BlockSpec(shape, lambda *_: (0,) * nd, pipeline_mode=pl.Buffered(1))


def _params(sem):
    return pltpu.CompilerParams(dimension_semantics=sem, vmem_limit_bytes=VMEM_LIMIT)


def _mem_kv_kernel(mem_ref, g_ref, w_ref, o_ref):
    n = _rms(mem_ref[...], g_ref[...]).astype(BF16)
    o_ref[...] = _dot(n, w_ref[...]).astype(o_ref.dtype)


def _mem_kv(mem, g_mem, w_xkv):
    B, M, D = mem.shape
    N = w_xkv.shape[1]
    return pl.pallas_call(
        _mem_kv_kernel,
        grid=(B,),
        in_specs=[pl.BlockSpec((None, M, D), lambda b: (b, 0, 0)),
                  _const_spec((1, D)), _const_spec((D, N))],
        out_specs=pl.BlockSpec((None, M, N), lambda b: (b, 0, 0)),
        out_shape=jax.ShapeDtypeStruct((B, M, N), BF16),
        compiler_params=_params(("parallel",)),
        name="mem_kv",
    )(mem, g_mem, w_xkv)


def _in_proj_kernel(x_ref, pos_ref, gmix_ref, win_ref, bgate_ref,
                    gq_ref, wuq_ref, gkv_ref, wuk_ref, wuv_ref, vones_ref, inv_ref,
                    q_ref, k_ref, vt_ref, rq_ref, rk_ref, rv_ref, rg_ref, gate_ref):
    lat_w = MLA_Q_RANK + MLA_KV_RANK + HEAD_PAD
    hw = RET_HEADS * RET_DK
    u = _rms(x_ref[...], gmix_ref[...]).astype(BF16)
    lat = _dot(u, win_ref[:, :lat_w])
    ret = _dot(u, win_ref[:, lat_w:lat_w + 4 * hw])
    gate_pre = _dot(u, win_ref[:, lat_w + 4 * hw:])

    ang = pos_ref[...].astype(F32) * inv_ref[...]
    cos_p, sin_p = jnp.cos(ang), jnp.sin(ang)
    lane = lax.broadcasted_iota(jnp.int32, ang.shape, 1)
    half_r = RET_DK // 2
    half_a = MLA_ROPE // 2
    cos_r = jnp.where(lane < half_r, cos_p, pltpu.roll(cos_p, half_r, 1))
    sin_r = jnp.where(lane < half_r, -sin_p, pltpu.roll(sin_p, half_r, 1))
    x1_a = jnp.logical_and(lane >= MLA_NOPE, lane < MLA_NOPE + half_a)
    x2_a = jnp.logical_and(lane >= MLA_NOPE + half_a, lane < MLA_NOPE + MLA_ROPE)
    cos_a = jnp.where(x1_a, cos_p, jnp.where(x2_a, pltpu.roll(cos_p, half_a, 1), 1.0))
    sin_a = jnp.where(x1_a, -sin_p, jnp.where(x2_a, pltpu.roll(sin_p, half_a, 1), 0.0))

    def rot_a(blk):
        partner = jnp.where(x1_a,
                            pltpu.roll(blk, HEAD_PAD - half_a, 1),
                            pltpu.roll(blk, half_a, 1))
        return blk * cos_a + partner * sin_a

    cq = _rms(lat[:, :MLA_Q_RANK], gq_ref[...]).astype(BF16)
    ckv = _rms(lat[:, MLA_Q_RANK:MLA_Q_RANK + MLA_KV_RANK], gkv_ref[...]).astype(BF16)
    kr = rot_a(lat[:, MLA_Q_RANK + MLA_KV_RANK:])

    qf = _dot(cq, wuq_ref[...])
    kf = _dot(ckv, wuk_ref[...])
    q_scale = (MLA_NOPE + MLA_ROPE) ** -0.5 * np.log2(np.e)
    for h in range(MLA_HEADS):
        sl = slice(h * HEAD_PAD, (h + 1) * HEAD_PAD)
        q_ref[:, sl] = (rot_a(qf[:, sl]) * q_scale).astype(q_ref.dtype)
        k_ref[:, sl] = (kf[:, sl] + kr).astype(k_ref.dtype)
    vt_ref[...] = (_dot_nt(wuv_ref[...], ckv) + vones_ref[...]).astype(vt_ref.dtype)

    k_scale = RET_DK ** -0.5
    for h in range(RET_HEADS):
        sl = slice(h * RET_DK, (h + 1) * RET_DK)
        bq = ret[:, sl]
        rq_ref[:, sl] = (bq * cos_r + pltpu.roll(bq, RET_DK // 2, 1) * sin_r).astype(rq_ref.dtype)
        bk = ret[:, hw + h * RET_DK: hw + (h + 1) * RET_DK]
        rk_ref[:, sl] = ((bk * cos_r + pltpu.roll(bk, RET_DK // 2, 1) * sin_r) * k_scale).astype(rk_ref.dtype)
    rv_ref[...] = ret[:, 2 * hw:3 * hw].astype(rv_ref.dtype)
    rg = ret[:, 3 * hw:]
    rg_ref[...] = (rg * jax.nn.sigmoid(rg)).astype(rg_ref.dtype)

    gate_ref[...] = jax.nn.sigmoid(gate_pre + bgate_ref[...]).astype(gate_ref.dtype)


def _in_proj(x2, pos2, g_mix, w_in_p, b_gate, g_q, w_uq, g_kv, w_uk, w_uv, v_ones,
             inv_pack, tm, tkv):
    T, D = x2.shape
    hw = RET_HEADS * RET_DK
    row = lambda n: pl.BlockSpec((tm, n), lambda i: (i, 0))
    consts = [g_mix, w_in_p, b_gate, g_q, w_uq, g_kv, w_uk, w_uv, v_ones, inv_pack]
    out_widths = [MLA_HEADS * HEAD_PAD, MLA_HEADS * HEAD_PAD, hw, hw, hw, hw, 2 * D]
    out_dtypes = [BF16, BF16, F32, F32, BF16, BF16, BF16]
    vw = MLA_HEADS * V_ROWS
    per = tkv // tm
    vt_spec = pl.BlockSpec((None, vw, tm), lambda i: (i // per, 0, i % per))
    vt_shape = jax.ShapeDtypeStruct((T // tkv, vw, tkv), BF16)
    out_specs = [row(n) for n in out_widths]
    out_shape = [jax.ShapeDtypeStruct((T, n), dt) for n, dt in zip(out_widths, out_dtypes)]
    return pl.pallas_call(
        _in_proj_kernel,
        grid=(T // tm,),
        in_specs=[row(D), row(1)] + [_const_spec(c.shape) for c in consts],
        out_specs=out_specs[:2] + [vt_spec] + out_specs[2:],
        out_shape=out_shape[:2] + [vt_shape] + out_shape[2:],
        compiler_params=_params(("parallel",)),
        name="in_proj",
    )(x2, pos2, *consts)


def _mla_attn_kernel(q_ref, k_ref, vt_ref, o_ref, s_ref, mblk_ref, m_ref, acc_ref, *, tq):
    i = pl.program_id(2)
    heads = q_ref.shape[1] // HEAD_PAD

    m_ref[...] = jnp.full(m_ref.shape, -jnp.inf, F32)
    acc_ref[...] = jnp.zeros(acc_ref.shape, F32)

    def scores(j, slot):
        k0 = pl.multiple_of(j * tq, tq)
        for h in range(heads):
            sl = slice(h * HEAD_PAD, (h + 1) * HEAD_PAD)
            s = _dot_nt(k_ref[pl.ds(k0, tq), sl], q_ref[:, sl])
            s_ref[slot, h] = s
            mblk_ref[slot, h] = jnp.max(s, axis=0, keepdims=True)

    def consume(j, slot, diagonal):
        for h in range(heads):
            s = s_ref[slot, h]
            if diagonal:
                key = lax.broadcasted_iota(jnp.int32, s.shape, 0)
                qry = lax.broadcasted_iota(jnp.int32, s.shape, 1)
                s = jnp.where(key <= qry, s, -jnp.inf)
                m_blk = jnp.max(s, axis=0, keepdims=True)
            else:
                m_blk = mblk_ref[slot, h]
            m_old = m_ref[h]
            m_new = jnp.maximum(m_old, m_blk)
            alpha = jnp.exp2(m_old - m_new)
            p = jnp.exp2(s - m_new).astype(BF16)
            vt = vt_ref[j, h * V_ROWS:(h + 1) * V_ROWS, :]
            acc_ref[h] = alpha * acc_ref[h] + _dot(vt, p)
            m_ref[h] = m_new

    scores(0, 0)

    def run(first, count, with_diagonal):
        for d in range(count):
            scores(first + d + 1, (d + 1) % 2)
            consume(first + d, d % 2, False)
        if with_diagonal:
            consume(first + count, count % 2, True)

    def body(t, carry):
        run(LOOP_BLOCKS * t, LOOP_BLOCKS, False)
        return carry

    lax.fori_loop(0, i // LOOP_BLOCKS, body, 0)
    for r in range(LOOP_BLOCKS):
        @pl.when(i % LOOP_BLOCKS == r)
        def _(r=r):
            run(i - r, r, True)

    out_t = jnp.concatenate([acc_ref[h, :MLA_V] / acc_ref[h, MLA_V:MLA_V + 1] for h in range(heads)], axis=0)
    o_ref[...] = out_t.T.astype(o_ref.dtype)


def _mla_attn(q, k, vt, tq, heads_per_step=2):
    B, S, _ = q.shape
    groups = MLA_HEADS // heads_per_step
    qw = heads_per_step * HEAD_PAD
    return pl.pallas_call(
        functools.partial(_mla_attn_kernel, tq=tq),
        grid=(B, groups, S // tq),
        in_specs=[pl.BlockSpec((None, tq, qw), lambda b, g, i: (b, i, g)),
                  pl.BlockSpec((None, S, qw), lambda b, g, i: (b, 0, g)),
                  pl.BlockSpec((None, S // tq, heads_per_step * V_ROWS, tq), lambda b, g, i: (b, 0, g, 0))],
        out_specs=pl.BlockSpec((None, tq, heads_per_step * MLA_V), lambda b, g, i: (b, i, g)),
        out_shape=jax.ShapeDtypeStruct((B, S, MLA_HEADS * MLA_V), BF16),
        scratch_shapes=[pltpu.VMEM((2, heads_per_step, tq, tq), F32),
                        pltpu.VMEM((2, heads_per_step, 1, tq), F32),
                        pltpu.VMEM((heads_per_step, 1, tq), F32),
                        pltpu.VMEM((heads_per_step, V_ROWS, tq), F32)],
        compiler_params=_params(("parallel", "parallel", "parallel")),
        name="mla_attn",
    )(q, k, vt)


def _retention_tables():
    C = RET_CHUNK
    log_g = np.log1p(-np.exp2(-5.0 - np.arange(RET_HEADS, dtype=np.float64)))
    idx = np.arange(C, dtype=np.float64)
    rel = idx[:, None] - idx[None, :]
    dmask = np.where(rel >= 0, np.exp(log_g[:, None, None] * np.maximum(rel, 0.0)), 0.0)
    zeta = np.exp(log_g[:, None] * (C - 1.0 - idx)[None, :])
    xi = np.exp(log_g[:, None] * (idx + 1.0)[None, :])
    decay = np.exp(log_g * C)
    f = lambda a: jnp.asarray(a, F32)
    return f(dmask), f(zeta[:, :, None]), f(xi[:, :, None]), f(np.broadcast_to(decay[:, None, None], (RET_HEADS, 1, LANES)))


def _retention_kernel(q_ref, k_ref, v_ref, rg_ref, gret_ref, dmask_ref, zeta_ref, xi_ref, decay_ref,
                      o_ref, state_ref, *, chunks):
    @pl.when(pl.program_id(1) == 0)
    def _():
        state_ref[...] = jnp.zeros(state_ref.shape, F32)

    C = RET_CHUNK
    for h in range(RET_HEADS):
        sl = slice(h * RET_DK, (h + 1) * RET_DK)
        dmask, zeta, xi = dmask_ref[h], zeta_ref[h], xi_ref[h]
        decay = decay_ref[h][:, :RET_DV]
        inner, kv, qx = [], [], []
        for c in range(chunks):
            rows = slice(c * C, (c + 1) * C)
            q, k, v = q_ref[rows, sl], k_ref[rows, sl], v_ref[rows, sl]
            s = (_dot_nt(q.astype(BF16), k.astype(BF16)) * dmask).astype(BF16)
            inner.append(_dot(s, v))
            kv.append(_dot_tn((k * zeta).astype(BF16), v))
            qx.append((q * xi).astype(BF16))
        state = state_ref[h]
        for c in range(chunks):
            rows = slice(c * C, (c + 1) * C)
            y = inner[c] + _dot(qx[c], state.astype(BF16))
            state = state * decay + kv[c]
            mu = jnp.mean(y, axis=-1, keepdims=True)
            yc = y - mu
            var = jnp.mean(yc * yc, axis=-1, keepdims=True)
            yn = yc * lax.rsqrt(var + EPS) * gret_ref[:, sl]
            o_ref[rows, sl] = (rg_ref[rows, sl].astype(F32) * yn).astype(o_ref.dtype)
        state_ref[h] = state


def _retention(rq, rk, rv, rg, g_ret, tr):
    B, S, W = rq.shape
    tables = _retention_tables()
    blk = pl.BlockSpec((None, tr, W), lambda b, i: (b, i, 0))
    return pl.pallas_call(
        functools.partial(_retention_kernel, chunks=tr // RET_CHUNK),
        grid=(B, S // tr),
        in_specs=[blk, blk, blk, blk, _const_spec(g_ret.shape)] + [_const_spec(t.shape) for t in tables],
        out_specs=blk,
        out_shape=jax.ShapeDtypeStruct((B, S, W), BF16),
        scratch_shapes=[pltpu.VMEM((RET_HEADS, RET_DK, RET_DV), F32)],
        compiler_params=_params(("parallel", "arbitrary")),
        name="retention",
    )(rq, rk, rv, rg, g_ret, *tables)


def _mix_cross_kernel(x_ref, oa_ref, yr_ref, gate_ref, wpa_ref, wpr_ref, wout_ref, gc_ref, wxq_ref,
                      mkv_ref, wxo_ref, h_ref):
    D = x_ref.shape[1]
    hd = D // X_HEADS
    y_a = _dot(oa_ref[...], wpa_ref[...])
    y_r = _dot(yr_ref[...], wpr_ref[...])
    merged = gate_ref[:, :D].astype(F32) * y_a + gate_ref[:, D:].astype(F32) * y_r
    h1 = x_ref[...] + _dot(merged.astype(BF16), wout_ref[...])

    xq = _dot(_rms(h1, gc_ref[...]).astype(BF16), wxq_ref[...]).astype(BF16)
    scale = hd ** -0.5
    outs = []
    for h in range(X_HEADS):
        sl = slice(h * hd, (h + 1) * hd)
        s = _dot_nt(xq[:, sl], mkv_ref[:, sl]) * scale
        e = jnp.exp(s - jnp.max(s, axis=-1, keepdims=True))
        o = _dot(e.astype(BF16), mkv_ref[:, D + h * hd: D + (h + 1) * hd])
        outs.append((o / jnp.sum(e, axis=-1, keepdims=True)).astype(BF16))
    xo = jnp.concatenate(outs, axis=-1)
    h_ref[...] = h1 + _dot(xo, wxo_ref[...])


def _mix_cross(x2, o_a, y_rg, gates, w_pa, w_pr, w_out, g_cross, w_xq, mkv, w_xo, tm, seq):
    T, D = x2.shape
    M = mkv.shape[1]
    row = lambda n: pl.BlockSpec((tm, n), lambda i: (i, 0))
    steps_per_batch = seq // tm
    return pl.pallas_call(
        _mix_cross_kernel,
        grid=(T // tm,),
        in_specs=[row(D), row(o_a.shape[1]), row(y_rg.shape[1]), row(gates.shape[1]),
                  _const_spec(w_pa.shape), _const_spec(w_pr.shape), _const_spec(w_out.shape),
                  _const_spec(g_cross.shape), _const_spec(w_xq.shape),
                  pl.BlockSpec((None, M, 2 * D), lambda i: (i // steps_per_batch, 0, 0)),
                  _const_spec(w_xo.shape)],
        out_specs=row(D),
        out_shape=jax.ShapeDtypeStruct((T, D), F32),
        compiler_params=_params(("parallel",)),
        name="mix_cross",
    )(x2, o_a, y_rg, gates, w_pa, w_pr, w_out, g_cross, w_xq, mkv, w_xo)


def _ffn_kernel(h_ref, gf_ref, wu_ref, wc_ref, bc_ref, wd_ref, gfin_ref,
                o_ref, n_ref, act_ref, bufa_ref, bufb_ref, haloa_ref, halob_ref, *, steps_per_batch):
    tm = h_ref.shape[0]
    nch, _, fc = haloa_ref.shape
    d_ff = nch * fc
    halo = SUBLANES

    @pl.when(pl.program_id(0) % steps_per_batch == 0)
    def _():
        haloa_ref[...] = jnp.zeros(haloa_ref.shape, F32)
        halob_ref[...] = jnp.zeros(halob_ref.shape, F32)

    n_ref[...] = _rms(h_ref[...], gf_ref[...]).astype(BF16)

    def conv(buf_ref, halo_ref, col, up, j):
        buf_ref[:halo, :] = halo_ref[j]
        buf_ref[halo:, :] = up
        halo_ref[j] = up[tm - halo:, :]
        w = wc_ref[:, col:col + fc]
        out = up * w[CONV_W - 1:CONV_W, :] + bc_ref[:, col:col + fc]
        for t in range(1, CONV_W):
            out = out + buf_ref[halo - t: halo - t + tm, :] * w[CONV_W - 1 - t:CONV_W - t, :]
        return out

    n = n_ref[...]
    up_pair = lambda j: (_dot(n, wu_ref[:, j * fc:(j + 1) * fc]), _dot(n, wu_ref[:, d_ff + j * fc:d_ff + (j + 1) * fc]))
    ups = up_pair(0)
    for j in range(nch):
        ua, ub = ups
        if j + 1 < nch:
            ups = up_pair(j + 1)
        a = conv(bufa_ref.at[j % 2], haloa_ref, j * fc, ua, j)
        b = conv(bufb_ref.at[j % 2], halob_ref, d_ff + j * fc, ub, j)
        act_ref[:, j * fc:(j + 1) * fc] = (a * jax.nn.sigmoid(a) * b).astype(BF16)

    o_ref[...] = _rms(h_ref[...] + _dot(act_ref[...], wd_ref[...]), gfin_ref[...])


def _ffn_out(h2, g_ffn, w_u, w_c, b_c, w_d, g_final, tm, seq):
    T, D = h2.shape
    fc = FF_CHUNK
    nch = w_d.shape[0] // fc
    row = pl.BlockSpec((tm, D), lambda i: (i, 0))
    consts = [g_ffn, w_u, w_c, b_c, w_d, g_final]
    return pl.pallas_call(
        functools.partial(_ffn_kernel, steps_per_batch=seq // tm),
        grid=(T // tm,),
        in_specs=[row] + [_const_spec(c.shape) for c in consts],
        out_specs=row,
        out_shape=jax.ShapeDtypeStruct((T, D), F32),
        scratch_shapes=[pltpu.VMEM((tm, D), BF16), pltpu.VMEM((tm, nch * fc), BF16),
                        pltpu.VMEM((2, tm + SUBLANES, fc), F32), pltpu.VMEM((2, tm + SUBLANES, fc), F32),
                        pltpu.VMEM((nch, SUBLANES, fc), F32), pltpu.VMEM((nch, SUBLANES, fc), F32)],
        compiler_params=_params(("arbitrary",)),
        name="ffn_out",
    )(h2, *consts)


def _rope_inv(half):
    return ROPE_THETA ** (-np.arange(half, dtype=np.float32) / np.float32(half))


def _mla_column_maps():
    half = MLA_ROPE // 2
    used = MLA_NOPE + MLA_ROPE
    q_cols = np.zeros((MLA_HEADS, HEAD_PAD), np.int32)
    k_cols = np.zeros((MLA_HEADS, HEAD_PAD), np.int32)
    valid_q = np.zeros((MLA_HEADS, HEAD_PAD), bool)
    valid_k = np.zeros((MLA_HEADS, HEAD_PAD), bool)
    for h in range(MLA_HEADS):
        q_cols[h, :used] = h * used + np.arange(used)
        valid_q[h, :used] = True
        k_cols[h, :MLA_NOPE] = h * (MLA_NOPE + MLA_V) + np.arange(MLA_NOPE)
        valid_k[h, :MLA_NOPE] = True
    v_cols = np.zeros((MLA_HEADS, V_ROWS), np.int32)
    valid_v = np.zeros((MLA_HEADS, V_ROWS), bool)
    for h in range(MLA_HEADS):
        v_cols[h, :MLA_V] = h * (MLA_NOPE + MLA_V) + MLA_NOPE + np.arange(MLA_V)
        valid_v[h, :MLA_V] = True
    del half
    return (q_cols.reshape(-1), valid_q.reshape(-1), k_cols.reshape(-1), valid_k.reshape(-1),
            v_cols.reshape(-1), valid_v.reshape(-1))


def kernel(x, mem, positions, g_mix, w_in, b_gate, g_q_lat, w_uq, g_kv_lat, w_ukv, w_proj_mla, g_ret,
           w_proj_ret, w_out, g_cross, g_mem, w_xq, w_xkv, w_xo, g_ffn, w_up, w_conv, b_conv, w_down, g_final):
    B, S, D = x.shape
    T = B * S
    depth = w_in.shape[0]
    d_ff = w_down.shape[1]
    hw = RET_HEADS * RET_DK
    tm = min(512, S)
    tq = min(512, S)
    tr = min(512, S)
    tf = min(512, S)

    half_a = MLA_ROPE // 2
    inv_pack = np.zeros((1, LANES), np.float32)
    inv_pack[0, :RET_DK // 2] = _rope_inv(RET_DK // 2)
    inv_pack[0, MLA_NOPE:MLA_NOPE + half_a] = _rope_inv(half_a)
    q_cols, valid_q, k_cols, valid_k, v_cols, valid_v = _mla_column_maps()
    v_ones = jnp.asarray(np.where(valid_v, 0.0, 1.0)[:, None], F32)

    h = x.reshape(T, D)
    pos2 = positions.reshape(T, 1)
    row2 = lambda a: a.reshape(1, -1)
    for l in range(depth):
        c0 = MLA_Q_RANK + MLA_KV_RANK
        c1 = c0 + MLA_ROPE
        w_in_p = jnp.concatenate(
            [w_in[l][:, :c0], jnp.zeros((D, MLA_NOPE), F32), w_in[l][:, c0:c1],
             jnp.zeros((D, HEAD_PAD - MLA_NOPE - MLA_ROPE), F32), w_in[l][:, c1:]], axis=1).astype(BF16)
        w_uq_p = jnp.where(valid_q[None, :], w_uq[l][:, q_cols], 0.0).astype(BF16)
        w_uk_p = jnp.where(valid_k[None, :], w_ukv[l][:, k_cols], 0.0).astype(BF16)
        w_uv = jnp.where(valid_v[None, :], w_ukv[l][:, v_cols], 0.0).T.astype(BF16)
        assert d_ff % FF_CHUNK == 0
        w_d = w_down[l].astype(BF16)

        mkv = _mem_kv(mem, row2(g_mem[l]), w_xkv[l].astype(BF16))
        q, k, vt, rq, rk, rv, rg, gates = _in_proj(
            h, pos2, row2(g_mix[l]), w_in_p, row2(b_gate[l]), row2(g_q_lat[l]), w_uq_p,
            row2(g_kv_lat[l]), w_uk_p, w_uv, v_ones, jnp.asarray(inv_pack), tm, tq)
        seq3 = lambda a: a.reshape(B, S, a.shape[-1])
        o_a = _mla_attn(seq3(q), seq3(k), vt.reshape(B, S // tq, vt.shape[1], tq), tq)
        y_rg = _retention(seq3(rq), seq3(rk), seq3(rv), seq3(rg), row2(g_ret[l]), tr)
        h = _mix_cross(h, o_a.reshape(T, -1), y_rg.reshape(T, -1), gates, w_proj_mla[l].astype(BF16),
                       w_proj_ret[l].astype(BF16), w_out[l].astype(BF16), row2(g_cross[l]),
                       w_xq[l].astype(BF16), mkv, w_xo[l].astype(BF16), tm, S)
        last = l == depth - 1
        assert last, "only a single layer stack is supported"
        h = _ffn_out(h, row2(g_ffn[l]), w_up[l].astype(BF16), w_conv[l], row2(b_conv[l]), w_d, row2(g_final), tf, S)
    return h.reshape(B, S, D)
```

```python
import functools

import numpy as np
import jax
import jax.numpy as jnp
from jax import lax
from jax.experimental import pallas as pl
from jax.experimental.pallas import tpu as pltpu

MLA_HEADS = 8
MLA_NOPE = 64
MLA_ROPE = 32
MLA_V = 64
MLA_Q_RANK = 256
MLA_KV_RANK = 128
RET_HEADS = 4
RET_DK = 128
RET_DV = 128
RET_CHUNK = 128
X_HEADS = 4
CONV_W = 3
ROPE_THETA = 10000.0
EPS = 1e-6

LANES = 128
SUBLANES = 8
HEAD_PAD = 128
V_ROWS = MLA_V + 16
LOOP_BLOCKS = 8
FF_CHUNK = 256
VMEM_LIMIT = 56 * 1024 * 1024

BF16 = jnp.bfloat16
F32 = jnp.float32


def _rms(x, g):
    return x * lax.rsqrt(jnp.mean(x * x, axis=-1, keepdims=True) + EPS) * g


def _dot(a, b):
    return jnp.dot(a, b, preferred_element_type=F32)


def _dot_nt(a, b):
    return lax.dot_general(a, b, (((1,), (1,)), ((), ())), preferred_element_type=F32)


def _dot_tn(a, b):
    return lax.dot_general(a, b, (((0,), (0,)), ((), ())), preferred_element_type=F32)


def _const_spec(shape):
    nd = len(shape)
    return pl.BlockSpec(shape, lambda *_: (0,) * nd, pipeline_mode=pl.Buffered(1))


def _params(sem):
    return pltpu.CompilerParams(dimension_semantics=sem, vmem_limit_bytes=VMEM_LIMIT)


def _mem_kv_kernel(mem_ref, g_ref, w_ref, o_ref):
    n = _rms(mem_ref[...], g_ref[...]).astype(BF16)
    o_ref[...] = _dot(n, w_ref[...]).astype(o_ref.dtype)


def _mem_kv(mem, g_mem, w_xkv):
    B, M, D = mem.shape
    N = w_xkv.shape[1]
    return pl.pallas_call(
        _mem_kv_kernel,
        grid=(B,),
        in_specs=[pl.BlockSpec((None, M, D), lambda b: (b, 0, 0)),
                  _const_spec((1, D)), _const_spec((D, N))],
        out_specs=pl.BlockSpec((None, M, N), lambda b: (b, 0, 0)),
        out_shape=jax.ShapeDtypeStruct((B, M, N), BF16),
        compiler_params=_params(("parallel",)),
        name="mem_kv",
    )(mem, g_mem, w_xkv)


def _w_in_prep_kernel(w_ref, o_ref):
    c0 = MLA_Q_RANK + MLA_KV_RANK
    c1 = c0 + MLA_ROPE
    rows = w_ref.shape[0]
    o_ref[:, :c0] = w_ref[:, :c0].astype(o_ref.dtype)
    kr = jnp.concatenate([jnp.zeros((rows, MLA_NOPE), F32), w_ref[:, c0:c1],
                          jnp.zeros((rows, HEAD_PAD - MLA_NOPE - MLA_ROPE), F32)], axis=1)
    o_ref[:, c0:c0 + HEAD_PAD] = kr.astype(o_ref.dtype)
    o_ref[:, c0 + HEAD_PAD:] = w_ref[:, c1:].astype(o_ref.dtype)


def _w_in_prep(w_in_l, tr=256):
    D, N = w_in_l.shape
    n_out = N - MLA_ROPE + HEAD_PAD
    return pl.pallas_call(
        _w_in_prep_kernel,
        grid=(D // tr,),
        in_specs=[pl.BlockSpec((tr, N), lambda i: (i, 0))],
        out_specs=pl.BlockSpec((tr, n_out), lambda i: (i, 0)),
        out_shape=jax.ShapeDtypeStruct((D, n_out), BF16),
        compiler_params=_params(("parallel",)),
        name="w_in_prep",
    )(w_in_l)


def _in_proj_kernel(x_ref, pos_ref, gmix_ref, win_ref, bgate_ref,
                    gq_ref, wuq_ref, gkv_ref, wuk_ref, wuv_ref, vones_ref, inv_ref,
                    q_ref, k_ref, vt_ref, rq_ref, rk_ref, rv_ref, rg_ref, gate_ref):
    lat_w = MLA_Q_RANK + MLA_KV_RANK + HEAD_PAD
    hw = RET_HEADS * RET_DK
    u = _rms(x_ref[...], gmix_ref[...]).astype(BF16)
    lat = _dot(u, win_ref[:, :lat_w])
    ret = _dot(u, win_ref[:, lat_w:lat_w + 4 * hw])
    gate_pre = _dot(u, win_ref[:, lat_w + 4 * hw:])

    ang = pos_ref[...].astype(F32) * inv_ref[...]
    cos_p, sin_p = jnp.cos(ang), jnp.sin(ang)
    lane = lax.broadcasted_iota(jnp.int32, ang.shape, 1)
    half_r = RET_DK // 2
    half_a = MLA_ROPE // 2
    cos_r = jnp.where(lane < half_r, cos_p, pltpu.roll(cos_p, half_r, 1))
    sin_r = jnp.where(lane < half_r, -sin_p, pltpu.roll(sin_p, half_r, 1))
    x1_a = jnp.logical_and(lane >= MLA_NOPE, lane < MLA_NOPE + half_a)
    x2_a = jnp.logical_and(lane >= MLA_NOPE + half_a, lane < MLA_NOPE + MLA_ROPE)
    cos_a = jnp.where(x1_a, cos_p, jnp.where(x2_a, pltpu.roll(cos_p, half_a, 1), 1.0))
    sin_a = jnp.where(x1_a, -sin_p, jnp.where(x2_a, pltpu.roll(sin_p, half_a, 1), 0.0))

    def rot_a(blk):
        partner = jnp.where(x1_a,
                            pltpu.roll(blk, HEAD_PAD - half_a, 1),
                            pltpu.roll(blk, half_a, 1))
        return blk * cos_a + partner * sin_a

    cq = _rms(lat[:, :MLA_Q_RANK], gq_ref[...]).astype(BF16)
    ckv = _rms(lat[:, MLA_Q_RANK:MLA_Q_RANK + MLA_KV_RANK], gkv_ref[...]).astype(BF16)
    kr = rot_a(lat[:, MLA_Q_RANK + MLA_KV_RANK:])

    qf = _dot(cq, wuq_ref[...])
    kf = _dot(ckv, wuk_ref[...])
    q_scale = (MLA_NOPE + MLA_ROPE) ** -0.5 * np.log2(np.e)
    for h in range(MLA_HEADS):
        sl = slice(h * HEAD_PAD, (h + 1) * HEAD_PAD)
        q_ref[:, sl] = (rot_a(qf[:, sl]) * q_scale).astype(q_ref.dtype)
        k_ref[:, sl] = (kf[:, sl] + kr).astype(k_ref.dtype)
    vt_ref[...] = (_dot_nt(wuv_ref[...], ckv) + vones_ref[...]).astype(vt_ref.dtype)

    k_scale = RET_DK ** -0.5
    for h in range(RET_HEADS):
        sl = slice(h * RET_DK, (h + 1) * RET_DK)
        bq = ret[:, sl]
        rq_ref[:, sl] = (bq * cos_r + pltpu.roll(bq, RET_DK // 2, 1) * sin_r).astype(rq_ref.dtype)
        bk = ret[:, hw + h * RET_DK: hw + (h + 1) * RET_DK]
        rk_ref[:, sl] = ((bk * cos_r + pltpu.roll(bk, RET_DK // 2, 1) * sin_r) * k_scale).astype(rk_ref.dtype)
    rv_ref[...] = ret[:, 2 * hw:3 * hw].astype(rv_ref.dtype)
    rg = ret[:, 3 * hw:]
    rg_ref[...] = (rg * jax.nn.sigmoid(rg)).astype(rg_ref.dtype)

    gate_ref[...] = jax.nn.sigmoid(gate_pre + bgate_ref[...]).astype(gate_ref.dtype)


def _in_proj(x2, pos2, g_mix, w_in_p, b_gate, g_q, w_uq, g_kv, w_uk, w_uv, v_ones,
             inv_pack, tm, tkv):
    T, D = x2.shape
    hw = RET_HEADS * RET_DK
    row = lambda n: pl.BlockSpec((tm, n), lambda i: (i, 0))
    consts = [g_mix, w_in_p, b_gate, g_q, w_uq, g_kv, w_uk, w_uv, v_ones, inv_pack]
    out_widths = [MLA_HEADS * HEAD_PAD, MLA_HEADS * HEAD_PAD, hw, hw, hw, hw, 2 * D]
    out_dtypes = [BF16, BF16, F32, F32, BF16, BF16, BF16]
    vw = MLA_HEADS * V_ROWS
    per = tkv // tm
    vt_spec = pl.BlockSpec((None, vw, tm), lambda i: (i // per, 0, i % per))
    vt_shape = jax.ShapeDtypeStruct((T // tkv, vw, tkv), BF16)
    out_specs = [row(n) for n in out_widths]
    out_shape = [jax.ShapeDtypeStruct((T, n), dt) for n, dt in zip(out_widths, out_dtypes)]
    return pl.pallas_call(
        _in_proj_kernel,
        grid=(T // tm,),
        in_specs=[row(D), row(1)] + [_const_spec(c.shape) for c in consts],
        out_specs=out_specs[:2] + [vt_spec] + out_specs[2:],
        out_shape=out_shape[:2] + [vt_shape] + out_shape[2:],
        compiler_params=_params(("parallel",)),
        name="in_proj",
    )(x2, pos2, *consts)


def _mla_attn_kernel(q_ref, k_ref, vt_ref, o_ref, s_ref, mblk_ref, m_ref, acc_ref, *, tq):
    i = pl.program_id(2)
    heads = q_ref.shape[1] // HEAD_PAD

    m_ref[...] = jnp.full(m_ref.shape, -jnp.inf, F32)
    acc_ref[...] = jnp.zeros(acc_ref.shape, F32)

    def scores(j, slot):
        k0 = pl.multiple_of(j * tq, tq)
        for h in range(heads):
            sl = slice(h * HEAD_PAD, (h + 1) * HEAD_PAD)
            s = _dot_nt(k_ref[pl.ds(k0, tq), sl], q_ref[:, sl])
            s_ref[slot, h] = s
            mblk_ref[slot, h] = jnp.max(s, axis=0, keepdims=True)

    def consume(j, slot, diagonal):
        for h in range(heads):
            s = s_ref[slot, h]
            if diagonal:
                key = lax.broadcasted_iota(jnp.int32, s.shape, 0)
                qry = lax.broadcasted_iota(jnp.int32, s.shape, 1)
                s = jnp.where(key <= qry, s, -jnp.inf)
                m_blk = jnp.max(s, axis=0, keepdims=True)
            else:
                m_blk = mblk_ref[slot, h]
            m_old = m_ref[h]
            m_new = jnp.maximum(m_old, m_blk)
            alpha = jnp.exp2(m_old - m_new)
            p = jnp.exp2(s - m_new).astype(BF16)
            vt = vt_ref[j, h * V_ROWS:(h + 1) * V_ROWS, :]
            acc_ref[h] = alpha * acc_ref[h] + _dot(vt, p)
            m_ref[h] = m_new

    scores(0, 0)

    def run(first, count, with_diagonal):
        for d in range(count):
            scores(first + d + 1, (d + 1) % 2)
            consume(first + d, d % 2, False)
        if with_diagonal:
            consume(first + count, count % 2, True)

    def body(t, carry):
        run(LOOP_BLOCKS * t, LOOP_BLOCKS, False)
        return carry

    lax.fori_loop(0, i // LOOP_BLOCKS, body, 0)
    for r in range(LOOP_BLOCKS):
        @pl.when(i % LOOP_BLOCKS == r)
        def _(r=r):
            run(i - r, r, True)

    out_t = jnp.concatenate([acc_ref[h, :MLA_V] / acc_ref[h, MLA_V:MLA_V + 1] for h in range(heads)], axis=0)
    o_ref[...] = out_t.T.astype(o_ref.dtype)


def _mla_attn(q, k, vt, tq, heads_per_step=2):
    B, S, _ = q.shape
    groups = MLA_HEADS // heads_per_step
    qw = heads_per_step * HEAD_PAD
    return pl.pallas_call(
        functools.partial(_mla_attn_kernel, tq=tq),
        grid=(B, groups, S // tq),
        in_specs=[pl.BlockSpec((None, tq, qw), lambda b, g, i: (b, i, g)),
                  pl.BlockSpec((None, S, qw), lambda b, g, i: (b, 0, g)),
                  pl.BlockSpec((None, S // tq, heads_per_step * V_ROWS, tq), lambda b, g, i: (b, 0, g, 0))],
        out_specs=pl.BlockSpec((None, tq, heads_per_step * MLA_V), lambda b, g, i: (b, i, g)),
        out_shape=jax.ShapeDtypeStruct((B, S, MLA_HEADS * MLA_V), BF16),
        scratch_shapes=[pltpu.VMEM((2, heads_per_step, tq, tq), F32),
                        pltpu.VMEM((2, heads_per_step, 1, tq), F32),
                        pltpu.VMEM((heads_per_step, 1, tq), F32),
                        pltpu.VMEM((heads_per_step, V_ROWS, tq), F32)],
        compiler_params=_params(("parallel", "parallel", "parallel")),
        name="mla_attn",
    )(q, k, vt)


def _retention_tables():
    C = RET_CHUNK
    log_g = np.log1p(-np.exp2(-5.0 - np.arange(RET_HEADS, dtype=np.float64)))
    idx = np.arange(C, dtype=np.float64)
    rel = idx[:, None] - idx[None, :]
    dmask = np.where(rel >= 0, np.exp(log_g[:, None, None] * np.maximum(rel, 0.0)), 0.0)
    zeta = np.exp(log_g[:, None] * (C - 1.0 - idx)[None, :])
    xi = np.exp(log_g[:, None] * (idx + 1.0)[None, :])
    decay = np.exp(log_g * C)
    f = lambda a: jnp.asarray(a, F32)
    return f(dmask), f(zeta[:, :, None]), f(xi[:, :, None]), f(np.broadcast_to(decay[:, None, None], (RET_HEADS, 1, LANES)))


def _retention_kernel(q_ref, k_ref, v_ref, rg_ref, gret_ref, dmask_ref, zeta_ref, xi_ref, decay_ref,
                      o_ref, state_ref, *, chunks):
    @pl.when(pl.program_id(1) == 0)
    def _():
        state_ref[...] = jnp.zeros(state_ref.shape, F32)

    C = RET_CHUNK
    for h in range(RET_HEADS):
        sl = slice(h * RET_DK, (h + 1) * RET_DK)
        dmask, zeta, xi = dmask_ref[h], zeta_ref[h], xi_ref[h]
        decay = decay_ref[h][:, :RET_DV]
        inner, kv, qx = [], [], []
        for c in range(chunks):
            rows = slice(c * C, (c + 1) * C)
            q, k, v = q_ref[rows, sl], k_ref[rows, sl], v_ref[rows, sl]
            s = (_dot_nt(q.astype(BF16), k.astype(BF16)) * dmask).astype(BF16)
            inner.append(_dot(s, v))
            kv.append(_dot_tn((k * zeta).astype(BF16), v))
            qx.append((q * xi).astype(BF16))
        state = state_ref[h]
        for c in range(chunks):
            rows = slice(c * C, (c + 1) * C)
            y = inner[c] + _dot(qx[c], state.astype(BF16))
            state = state * decay + kv[c]
            mu = jnp.mean(y, axis=-1, keepdims=True)
            yc = y - mu
            var = jnp.mean(yc * yc, axis=-1, keepdims=True)
            yn = yc * lax.rsqrt(var + EPS) * gret_ref[:, sl]
            o_ref[rows, sl] = (rg_ref[rows, sl].astype(F32) * yn).astype(o_ref.dtype)
        state_ref[h] = state


def _retention(rq, rk, rv, rg, g_ret, tr):
    B, S, W = rq.shape
    tables = _retention_tables()
    blk = pl.BlockSpec((None, tr, W), lambda b, i: (b, i, 0))
    return pl.pallas_call(
        functools.partial(_retention_kernel, chunks=tr // RET_CHUNK),
        grid=(B, S // tr),
        in_specs=[blk, blk, blk, blk, _const_spec(g_ret.shape)] + [_const_spec(t.shape) for t in tables],
        out_specs=blk,
        out_shape=jax.ShapeDtypeStruct((B, S, W), BF16),
        scratch_shapes=[pltpu.VMEM((RET_HEADS, RET_DK, RET_DV), F32)],
        compiler_params=_params(("parallel", "arbitrary")),
        name="retention",
    )(rq, rk, rv, rg, g_ret, *tables)


def _mix_cross_kernel(x_ref, oa_ref, yr_ref, gate_ref, wpa_ref, wpr_ref, wout_ref, gc_ref, wxq_ref,
                      mkv_ref, wxo_ref, h_ref):
    D = x_ref.shape[1]
    hd = D // X_HEADS
    y_a = _dot(oa_ref[...], wpa_ref[...])
    y_r = _dot(yr_ref[...], wpr_ref[...])
    merged = gate_ref[:, :D].astype(F32) * y_a + gate_ref[:, D:].astype(F32) * y_r
    h1 = x_ref[...] + _dot(merged.astype(BF16), wout_ref[...])

    xq = _dot(_rms(h1, gc_ref[...]).astype(BF16), wxq_ref[...]).astype(BF16)
    scale = hd ** -0.5
    outs = []
    for h in range(X_HEADS):
        sl = slice(h * hd, (h + 1) * hd)
        s = _dot_nt(xq[:, sl], mkv_ref[:, sl]) * scale
        e = jnp.exp(s - jnp.max(s, axis=-1, keepdims=True))
        o = _dot(e.astype(BF16), mkv_ref[:, D + h * hd: D + (h + 1) * hd])
        outs.append((o / jnp.sum(e, axis=-1, keepdims=True)).astype(BF16))
    xo = jnp.concatenate(outs, axis=-1)
    h_ref[...] = h1 + _dot(xo, wxo_ref[...])


def _mix_cross(x2, o_a, y_rg, gates, w_pa, w_pr, w_out, g_cross, w_xq, mkv, w_xo, tm, seq):
    T, D = x2.shape
    M = mkv.shape[1]
    row = lambda n: pl.BlockSpec((tm, n), lambda i: (i, 0))
    steps_per_batch = seq // tm
    return pl.pallas_call(
        _mix_cross_kernel,
        grid=(T // tm,),
        in_specs=[row(D), row(o_a.shape[1]), row(y_rg.shape[1]), row(gates.shape[1]),
                  _const_spec(w_pa.shape), _const_spec(w_pr.shape), _const_spec(w_out.shape),
                  _const_spec(g_cross.shape), _const_spec(w_xq.shape),
                  pl.BlockSpec((None, M, 2 * D), lambda i: (i // steps_per_batch, 0, 0)),
                  _const_spec(w_xo.shape)],
        out_specs=row(D),
        out_shape=jax.ShapeDtypeStruct((T, D), F32),
        compiler_params=_params(("parallel",)),
        name="mix_cross",
    )(x2, o_a, y_rg, gates, w_pa, w_pr, w_out, g_cross, w_xq, mkv, w_xo)


def _ffn_kernel(h_ref, gf_ref, wu_ref, wc_ref, bc_ref, wd_ref, gfin_ref,
                o_ref, n_ref, act_ref, bufa_ref, bufb_ref, haloa_ref, halob_ref, *, steps_per_batch):
    tm = h_ref.shape[0]
    nch, _, fc = haloa_ref.shape
    d_ff = nch * fc
    halo = SUBLANES

    @pl.when(pl.program_id(0) % steps_per_batch == 0)
    def _():
        haloa_ref[...] = jnp.zeros(haloa_ref.shape, F32)
        halob_ref[...] = jnp.zeros(halob_ref.shape, F32)

    n_ref[...] = _rms(h_ref[...], gf_ref[...]).astype(BF16)

    def conv(buf_ref, halo_ref, col, up, j):
        buf_ref[:halo, :] = halo_ref[j]
        buf_ref[halo:, :] = up
        halo_ref[j] = up[tm - halo:, :]
        w = wc_ref[:, col:col + fc]
        out = up * w[CONV_W - 1:CONV_W, :] + bc_ref[:, col:col + fc]
        for t in range(1, CONV_W):
            out = out + buf_ref[halo - t: halo - t + tm, :] * w[CONV_W - 1 - t:CONV_W - t, :]
        return out

    n = n_ref[...]
    up_pair = lambda j: (_dot(n, wu_ref[:, j * fc:(j + 1) * fc]), _dot(n, wu_ref[:, d_ff + j * fc:d_ff + (j + 1) * fc]))
    ups = up_pair(0)
    for j in range(nch):
        ua, ub = ups
        if j + 1 < nch:
            ups = up_pair(j + 1)
        a = conv(bufa_ref.at[j % 2], haloa_ref, j * fc, ua, j)
        b = conv(bufb_ref.at[j % 2], halob_ref, d_ff + j * fc, ub, j)
        act_ref[:, j * fc:(j + 1) * fc] = (a * jax.nn.sigmoid(a) * b).astype(BF16)

    o_ref[...] = _rms(h_ref[...] + _dot(act_ref[...], wd_ref[...]), gfin_ref[...])


def _ffn_out(h2, g_ffn, w_u, w_c, b_c, w_d, g_final, tm, seq):
    T, D = h2.shape
    fc = FF_CHUNK
    nch = w_d.shape[0] // fc
    row = pl.BlockSpec((tm, D), lambda i: (i, 0))
    consts = [g_ffn, w_u, w_c, b_c, w_d, g_final]
    return pl.pallas_call(
        functools.partial(_ffn_kernel, steps_per_batch=seq // tm),
        grid=(T // tm,),
        in_specs=[row] + [_const_spec(c.shape) for c in consts],
        out_specs=row,
        out_shape=jax.ShapeDtypeStruct((T, D), F32),
        scratch_shapes=[pltpu.VMEM((tm, D), BF16), pltpu.VMEM((tm, nch * fc), BF16),
                        pltpu.VMEM((2, tm + SUBLANES, fc), F32), pltpu.VMEM((2, tm + SUBLANES, fc), F32),
                        pltpu.VMEM((nch, SUBLANES, fc), F32), pltpu.VMEM((nch, SUBLANES, fc), F32)],
        compiler_params=_params(("arbitrary",)),
        name="ffn_out",
    )(h2, *consts)


def _rope_inv(half):
    return ROPE_THETA ** (-np.arange(half, dtype=np.float32) / np.float32(half))


def _mla_column_maps():
    half = MLA_ROPE // 2
    used = MLA_NOPE + MLA_ROPE
    q_cols = np.zeros((MLA_HEADS, HEAD_PAD), np.int32)
    k_cols = np.zeros((MLA_HEADS, HEAD_PAD), np.int32)
    valid_q = np.zeros((MLA_HEADS, HEAD_PAD), bool)
    valid_k = np.zeros((MLA_HEADS, HEAD_PAD), bool)
    for h in range(MLA_HEADS):
        q_cols[h, :used] = h * used + np.arange(used)
        valid_q[h, :used] = True
        k_cols[h, :MLA_NOPE] = h * (MLA_NOPE + MLA_V) + np.arange(MLA_NOPE)
        valid_k[h, :MLA_NOPE] = True
    v_cols = np.zeros((MLA_HEADS, V_ROWS), np.int32)
    valid_v = np.zeros((MLA_HEADS, V_ROWS), bool)
    for h in range(MLA_HEADS):
        v_cols[h, :MLA_V] = h * (MLA_NOPE + MLA_V) + MLA_NOPE + np.arange(MLA_V)
        valid_v[h, :MLA_V] = True
    del half
    return (q_cols.reshape(-1), valid_q.reshape(-1), k_cols.reshape(-1), valid_k.reshape(-1),
            v_cols.reshape(-1), valid_v.reshape(-1))


def kernel(x, mem, positions, g_mix, w_in, b_gate, g_q_lat, w_uq, g_kv_lat, w_ukv, w_proj_mla, g_ret,
           w_proj_ret, w_out, g_cross, g_mem, w_xq, w_xkv, w_xo, g_ffn, w_up, w_conv, b_conv, w_down, g_final):
    B, S, D = x.shape
    T = B * S
    depth = w_in.shape[0]
    d_ff = w_down.shape[1]
    hw = RET_HEADS * RET_DK
    tm = min(512, S)
    tq = min(512, S)
    tr = min(512, S)
    tf = min(512, S)

    half_a = MLA_ROPE // 2
    inv_pack = np.zeros((1, LANES), np.float32)
    inv_pack[0, :RET_DK // 2] = _rope_inv(RET_DK // 2)
    inv_pack[0, MLA_NOPE:MLA_NOPE + half_a] = _rope_inv(half_a)
    q_cols, valid_q, k_cols, valid_k, v_cols, valid_v = _mla_column_maps()
    v_ones = jnp.asarray(np.where(valid_v, 0.0, 1.0)[:, None], F32)

    h = x.reshape(T, D)
    pos2 = positions.reshape(T, 1)
    row2 = lambda a: a.reshape(1, -1)
    for l in range(depth):
        w_in_p = _w_in_prep(w_in[l])
        w_uq_p = jnp.where(valid_q[None, :], w_uq[l][:, q_cols], 0.0).astype(BF16)
        w_uk_p = jnp.where(valid_k[None, :], w_ukv[l][:, k_cols], 0.0).astype(BF16)
        w_uv = jnp.where(valid_v[None, :], w_ukv[l][:, v_cols], 0.0).T.astype(BF16)
        assert d_ff % FF_CHUNK == 0
        w_d = w_down[l].astype(BF16)

        mkv = _mem_kv(mem, row2(g_mem[l]), w_xkv[l].astype(BF16))
        q, k, vt, rq, rk, rv, rg, gates = _in_proj(
            h, pos2, row2(g_mix[l]), w_in_p, row2(b_gate[l]), row2(g_q_lat[l]), w_uq_p,
            row2(g_kv_lat[l]), w_uk_p, w_uv, v_ones, jnp.asarray(inv_pack), tm, tq)
        seq3 = lambda a: a.reshape(B, S, a.shape[-1])
        o_a = _mla_attn(seq3(q), seq3(k), vt.reshape(B, S // tq, vt.shape[1], tq), tq)
        y_rg = _retention(seq3(rq), seq3(rk), seq3(rv), seq3(rg), row2(g_ret[l]), tr)
        h = _mix_cross(h, o_a.reshape(T, -1), y_rg.reshape(T, -1), gates, w_proj_mla[l].astype(BF16),
                       w_proj_ret[l].astype(BF16), w_out[l].astype(BF16), row2(g_cross[l]),
                       w_xq[l].astype(BF16), mkv, w_xo[l].astype(BF16), tm, S)
        last = l == depth - 1
        assert last, "only a single layer stack is supported"
        h = _ffn_out(h, row2(g_ffn[l]), w_up[l].astype(BF16), w_conv[l], row2(b_conv[l]), w_d, row2(g_final), tf, S)
    return h.reshape(B, S, D)
```

```python
import functools

import numpy as np
import jax
import jax.numpy as jnp
from jax import lax
from jax.experimental import pallas as pl
from jax.experimental.pallas import tpu as pltpu

MLA_HEADS = 8
MLA_NOPE = 64
MLA_ROPE = 32
MLA_V = 64
MLA_Q_RANK = 256
MLA_KV_RANK = 128
RET_HEADS = 4
RET_DK = 128
RET_DV = 128
RET_CHUNK = 128
X_HEADS = 4
CONV_W = 3
ROPE_THETA = 10000.0
EPS = 1e-6

LANES = 128
SUBLANES = 8
HEAD_PAD = 128
V_ROWS = MLA_V + 16
LOOP_BLOCKS = 8
FF_CHUNK = 256
VMEM_LIMIT = 56 * 1024 * 1024

BF16 = jnp.bfloat16
F32 = jnp.float32


def _rms(x, g):
    return x * lax.rsqrt(jnp.mean(x * x, axis=-1, keepdims=True) + EPS) * g


def _dot(a, b):
    return jnp.dot(a, b, preferred_element_type=F32)


def _dot_nt(a, b):
    return lax.dot_general(a, b, (((1,), (1,)), ((), ())), preferred_element_type=F32)


def _dot_tn(a, b):
    return lax.dot_general(a, b, (((0,), (0,)), ((), ())), preferred_element_type=F32)


def _const_spec(shape):
    nd = len(shape)
    return pl.BlockSpec(shape, lambda *_: (0,) * nd, pipeline_mode=pl.Buffered(1))


def _params(sem):
    return pltpu.CompilerParams(dimension_semantics=sem, vmem_limit_bytes=VMEM_LIMIT)


def _mem_kv_kernel(mem_ref, g_ref, w_ref, o_ref):
    n = _rms(mem_ref[...], g_ref[...]).astype(BF16)
    o_ref[...] = _dot(n, w_ref[...]).astype(o_ref.dtype)


def _mem_kv(mem, g_mem, w_xkv):
    B, M, D = mem.shape
    N = w_xkv.shape[1]
    return pl.pallas_call(
        _mem_kv_kernel,
        grid=(B,),
        in_specs=[pl.BlockSpec((None, M, D), lambda b: (b, 0, 0)),
                  _const_spec((1, D)), _const_spec((D, N))],
        out_specs=pl.BlockSpec((None, M, N), lambda b: (b, 0, 0)),
        out_shape=jax.ShapeDtypeStruct((B, M, N), BF16),
        compiler_params=_params(("parallel",)),
        name="mem_kv",
    )(mem, g_mem, w_xkv)


def _w_in_prep_kernel(wt_ref, o_ref):
    c0 = MLA_Q_RANK + MLA_KV_RANK
    c1 = c0 + MLA_ROPE
    cols = wt_ref.shape[1]
    o_ref[:c0, :] = wt_ref[:c0, :].astype(o_ref.dtype)
    o_ref[c0:c0 + MLA_NOPE, :] = jnp.zeros((MLA_NOPE, cols), o_ref.dtype)
    o_ref[c0 + MLA_NOPE:c0 + MLA_NOPE + MLA_ROPE, :] = wt_ref[c0:c1, :].astype(o_ref.dtype)
    o_ref[c0 + MLA_NOPE + MLA_ROPE:c0 + HEAD_PAD, :] = jnp.zeros((HEAD_PAD - MLA_NOPE - MLA_ROPE, cols), o_ref.dtype)
    o_ref[c0 + HEAD_PAD:, :] = wt_ref[c1:, :].astype(o_ref.dtype)


def _w_in_prep(w_in_t, tc=256):
    N, D = w_in_t.shape
    n_out = N - MLA_ROPE + HEAD_PAD
    return pl.pallas_call(
        _w_in_prep_kernel,
        grid=(D // tc,),
        in_specs=[pl.BlockSpec((N, tc), lambda i: (0, i))],
        out_specs=pl.BlockSpec((n_out, tc), lambda i: (0, i)),
        out_shape=jax.ShapeDtypeStruct((n_out, D), BF16),
        compiler_params=_params(("parallel",)),
        name="w_in_prep",
    )(w_in_t)


def _in_proj_kernel(x_ref, pos_ref, gmix_ref, win_ref, bgate_ref,
                    gq_ref, wuq_ref, gkv_ref, wuk_ref, wuv_ref, vones_ref, inv_ref,
                    q_ref, k_ref, vt_ref, rq_ref, rk_ref, rv_ref, rg_ref, gate_ref):
    lat_w = MLA_Q_RANK + MLA_KV_RANK + HEAD_PAD
    hw = RET_HEADS * RET_DK
    u = _rms(x_ref[...], gmix_ref[...]).astype(BF16)
    lat = _dot_nt(u, win_ref[:lat_w, :])
    ret = _dot_nt(u, win_ref[lat_w:lat_w + 4 * hw, :])
    gate_pre = _dot_nt(u, win_ref[lat_w + 4 * hw:, :])

    ang = pos_ref[...].astype(F32) * inv_ref[...]
    cos_p, sin_p = jnp.cos(ang), jnp.sin(ang)
    lane = lax.broadcasted_iota(jnp.int32, ang.shape, 1)
    half_r = RET_DK // 2
    half_a = MLA_ROPE // 2
    cos_r = jnp.where(lane < half_r, cos_p, pltpu.roll(cos_p, half_r, 1))
    sin_r = jnp.where(lane < half_r, -sin_p, pltpu.roll(sin_p, half_r, 1))
    x1_a = jnp.logical_and(lane >= MLA_NOPE, lane < MLA_NOPE + half_a)
    x2_a = jnp.logical_and(lane >= MLA_NOPE + half_a, lane < MLA_NOPE + MLA_ROPE)
    cos_a = jnp.where(x1_a, cos_p, jnp.where(x2_a, pltpu.roll(cos_p, half_a, 1), 1.0))
    sin_a = jnp.where(x1_a, -sin_p, jnp.where(x2_a, pltpu.roll(sin_p, half_a, 1), 0.0))

    def rot_a(blk):
        partner = jnp.where(x1_a,
                            pltpu.roll(blk, HEAD_PAD - half_a, 1),
                            pltpu.roll(blk, half_a, 1))
        return blk * cos_a + partner * sin_a

    cq = _rms(lat[:, :MLA_Q_RANK], gq_ref[...]).astype(BF16)
    ckv = _rms(lat[:, MLA_Q_RANK:MLA_Q_RANK + MLA_KV_RANK], gkv_ref[...]).astype(BF16)
    kr = rot_a(lat[:, MLA_Q_RANK + MLA_KV_RANK:])

    qf = _dot(cq, wuq_ref[...])
    kf = _dot(ckv, wuk_ref[...])
    q_scale = (MLA_NOPE + MLA_ROPE) ** -0.5 * np.log2(np.e)
    for h in range(MLA_HEADS):
        sl = slice(h * HEAD_PAD, (h + 1) * HEAD_PAD)
        q_ref[:, sl] = (rot_a(qf[:, sl]) * q_scale).astype(q_ref.dtype)
        k_ref[:, sl] = (kf[:, sl] + kr).astype(k_ref.dtype)
    vt_ref[...] = (_dot_nt(wuv_ref[...], ckv) + vones_ref[...]).astype(vt_ref.dtype)

    k_scale = RET_DK ** -0.5
    for h in range(RET_HEADS):
        sl = slice(h * RET_DK, (h + 1) * RET_DK)
        bq = ret[:, sl]
        rq_ref[:, sl] = (bq * cos_r + pltpu.roll(bq, RET_DK // 2, 1) * sin_r).astype(rq_ref.dtype)
        bk = ret[:, hw + h * RET_DK: hw + (h + 1) * RET_DK]
        rk_ref[:, sl] = ((bk * cos_r + pltpu.roll(bk, RET_DK // 2, 1) * sin_r) * k_scale).astype(rk_ref.dtype)
    rv_ref[...] = ret[:, 2 * hw:3 * hw].astype(rv_ref.dtype)
    rg = ret[:, 3 * hw:]
    rg_ref[...] = (rg * jax.nn.sigmoid(rg)).astype(rg_ref.dtype)

    gate_ref[...] = jax.nn.sigmoid(gate_pre + bgate_ref[...]).astype(gate_ref.dtype)


def _in_proj(x2, pos2, g_mix, w_in_p, b_gate, g_q, w_uq, g_kv, w_uk, w_uv, v_ones,
             inv_pack, tm, tkv):
    T, D = x2.shape
    hw = RET_HEADS * RET_DK
    row = lambda n: pl.BlockSpec((tm, n), lambda i: (i, 0))
    consts = [g_mix, w_in_p, b_gate, g_q, w_uq, g_kv, w_uk, w_uv, v_ones, inv_pack]
    out_widths = [MLA_HEADS * HEAD_PAD, MLA_HEADS * HEAD_PAD, hw, hw, hw, hw, 2 * D]
    out_dtypes = [BF16, BF16, F32, F32, BF16, BF16, BF16]
    vw = MLA_HEADS * V_ROWS
    per = tkv // tm
    vt_spec = pl.BlockSpec((None, vw, tm), lambda i: (i // per, 0, i % per))
    vt_shape = jax.ShapeDtypeStruct((T // tkv, vw, tkv), BF16)
    out_specs = [row(n) for n in out_widths]
    out_shape = [jax.ShapeDtypeStruct((T, n), dt) for n, dt in zip(out_widths, out_dtypes)]
    return pl.pallas_call(
        _in_proj_kernel,
        grid=(T // tm,),
        in_specs=[row(D), row(1)] + [_const_spec(c.shape) for c in consts],
        out_specs=out_specs[:2] + [vt_spec] + out_specs[2:],
        out_shape=out_shape[:2] + [vt_shape] + out_shape[2:],
        compiler_params=_params(("parallel",)),
        name="in_proj",
    )(x2, pos2, *consts)


def _mla_attn_kernel(q_ref, k_ref, vt_ref, o_ref, s_ref, mblk_ref, m_ref, acc_ref, *, tq):
    i = pl.program_id(2)
    heads = q_ref.shape[1] // HEAD_PAD

    m_ref[...] = jnp.full(m_ref.shape, -jnp.inf, F32)
    acc_ref[...] = jnp.zeros(acc_ref.shape, F32)

    def scores(j, slot):
        k0 = pl.multiple_of(j * tq, tq)
        for h in range(heads):
            sl = slice(h * HEAD_PAD, (h + 1) * HEAD_PAD)
            s = _dot_nt(k_ref[pl.ds(k0, tq), sl], q_ref[:, sl])
            s_ref[slot, h] = s
            mblk_ref[slot, h] = jnp.max(s, axis=0, keepdims=True)

    def consume(j, slot, diagonal):
        for h in range(heads):
            s = s_ref[slot, h]
            if diagonal:
                key = lax.broadcasted_iota(jnp.int32, s.shape, 0)
                qry = lax.broadcasted_iota(jnp.int32, s.shape, 1)
                s = jnp.where(key <= qry, s, -jnp.inf)
                m_blk = jnp.max(s, axis=0, keepdims=True)
            else:
                m_blk = mblk_ref[slot, h]
            m_old = m_ref[h]
            m_new = jnp.maximum(m_old, m_blk)
            alpha = jnp.exp2(m_old - m_new)
            p = jnp.exp2(s - m_new).astype(BF16)
            vt = vt_ref[j, h * V_ROWS:(h + 1) * V_ROWS, :]
            acc_ref[h] = alpha * acc_ref[h] + _dot(vt, p)
            m_ref[h] = m_new

    scores(0, 0)

    def run(first, count, with_diagonal):
        for d in range(count):
            scores(first + d + 1, (d + 1) % 2)
            consume(first + d, d % 2, False)
        if with_diagonal:
            consume(first + count, count % 2, True)

    def body(t, carry):
        run(LOOP_BLOCKS * t, LOOP_BLOCKS, False)
        return carry

    lax.fori_loop(0, i // LOOP_BLOCKS, body, 0)
    for r in range(LOOP_BLOCKS):
        @pl.when(i % LOOP_BLOCKS == r)
        def _(r=r):
            run(i - r, r, True)

    out_t = jnp.concatenate([acc_ref[h, :MLA_V] / acc_ref[h, MLA_V:MLA_V + 1] for h in range(heads)], axis=0)
    o_ref[...] = out_t.T.astype(o_ref.dtype)


def _mla_attn(q, k, vt, tq, heads_per_step=2):
    B, S, _ = q.shape
    groups = MLA_HEADS // heads_per_step
    qw = heads_per_step * HEAD_PAD
    return pl.pallas_call(
        functools.partial(_mla_attn_kernel, tq=tq),
        grid=(B, groups, S // tq),
        in_specs=[pl.BlockSpec((None, tq, qw), lambda b, g, i: (b, i, g)),
                  pl.BlockSpec((None, S, qw), lambda b, g, i: (b, 0, g)),
                  pl.BlockSpec((None, S // tq, heads_per_step * V_ROWS, tq), lambda b, g, i: (b, 0, g, 0))],
        out_specs=pl.BlockSpec((None, tq, heads_per_step * MLA_V), lambda b, g, i: (b, i, g)),
        out_shape=jax.ShapeDtypeStruct((B, S, MLA_HEADS * MLA_V), BF16),
        scratch_shapes=[pltpu.VMEM((2, heads_per_step, tq, tq), F32),
                        pltpu.VMEM((2, heads_per_step, 1, tq), F32),
                        pltpu.VMEM((heads_per_step, 1, tq), F32),
                        pltpu.VMEM((heads_per_step, V_ROWS, tq), F32)],
        compiler_params=_params(("parallel", "parallel", "parallel")),
        name="mla_attn",
    )(q, k, vt)


def _retention_tables():
    C = RET_CHUNK
    log_g = np.log1p(-np.exp2(-5.0 - np.arange(RET_HEADS, dtype=np.float64)))
    idx = np.arange(C, dtype=np.float64)
    rel = idx[:, None] - idx[None, :]
    dmask = np.where(rel >= 0, np.exp(log_g[:, None, None] * np.maximum(rel, 0.0)), 0.0)
    zeta = np.exp(log_g[:, None] * (C - 1.0 - idx)[None, :])
    xi = np.exp(log_g[:, None] * (idx + 1.0)[None, :])
    decay = np.exp(log_g * C)
    f = lambda a: jnp.asarray(a, F32)
    return f(dmask), f(zeta[:, :, None]), f(xi[:, :, None]), f(np.broadcast_to(decay[:, None, None], (RET_HEADS, 1, LANES)))


def _retention_kernel(q_ref, k_ref, v_ref, rg_ref, gret_ref, dmask_ref, zeta_ref, xi_ref, decay_ref,
                      o_ref, state_ref, *, chunks):
    @pl.when(pl.program_id(1) == 0)
    def _():
        state_ref[...] = jnp.zeros(state_ref.shape, F32)

    C = RET_CHUNK
    for h in range(RET_HEADS):
        sl = slice(h * RET_DK, (h + 1) * RET_DK)
        dmask, zeta, xi = dmask_ref[h], zeta_ref[h], xi_ref[h]
        decay = decay_ref[h][:, :RET_DV]
        inner, kv, qx = [], [], []
        for c in range(chunks):
            rows = slice(c * C, (c + 1) * C)
            q, k, v = q_ref[rows, sl], k_ref[rows, sl], v_ref[rows, sl]
            s = (_dot_nt(q.astype(BF16), k.astype(BF16)) * dmask).astype(BF16)
            inner.append(_dot(s, v))
            kv.append(_dot_tn((k * zeta).astype(BF16), v))
            qx.append((q * xi).astype(BF16))
        state = state_ref[h]
        for c in range(chunks):
            rows = slice(c * C, (c + 1) * C)
            y = inner[c] + _dot(qx[c], state.astype(BF16))
            state = state * decay + kv[c]
            mu = jnp.mean(y, axis=-1, keepdims=True)
            yc = y - mu
            var = jnp.mean(yc * yc, axis=-1, keepdims=True)
            yn = yc * lax.rsqrt(var + EPS) * gret_ref[:, sl]
            o_ref[rows, sl] = (rg_ref[rows, sl].astype(F32) * yn).astype(o_ref.dtype)
        state_ref[h] = state


def _retention(rq, rk, rv, rg, g_ret, tr):
    B, S, W = rq.shape
    tables = _retention_tables()
    blk = pl.BlockSpec((None, tr, W), lambda b, i: (b, i, 0))
    return pl.pallas_call(
        functools.partial(_retention_kernel, chunks=tr // RET_CHUNK),
        grid=(B, S // tr),
        in_specs=[blk, blk, blk, blk, _const_spec(g_ret.shape)] + [_const_spec(t.shape) for t in tables],
        out_specs=blk,
        out_shape=jax.ShapeDtypeStruct((B, S, W), BF16),
        scratch_shapes=[pltpu.VMEM((RET_HEADS, RET_DK, RET_DV), F32)],
        compiler_params=_params(("parallel", "arbitrary")),
        name="retention",
    )(rq, rk, rv, rg, g_ret, *tables)


def _mix_cross_kernel(x_ref, oa_ref, yr_ref, gate_ref, wpa_ref, wpr_ref, wout_ref, gc_ref, wxq_ref,
                      mkv_ref, wxo_ref, h_ref):
    D = x_ref.shape[1]
    hd = D // X_HEADS
    y_a = _dot(oa_ref[...], wpa_ref[...])
    y_r = _dot(yr_ref[...], wpr_ref[...])
    merged = gate_ref[:, :D].astype(F32) * y_a + gate_ref[:, D:].astype(F32) * y_r
    h1 = x_ref[...] + _dot(merged.astype(BF16), wout_ref[...])

    xq = _dot(_rms(h1, gc_ref[...]).astype(BF16), wxq_ref[...]).astype(BF16)
    scale = hd ** -0.5
    outs = []
    for h in range(X_HEADS):
        sl = slice(h * hd, (h + 1) * hd)
        s = _dot_nt(xq[:, sl], mkv_ref[:, sl]) * scale
        e = jnp.exp(s - jnp.max(s, axis=-1, keepdims=True))
        o = _dot(e.astype(BF16), mkv_ref[:, D + h * hd: D + (h + 1) * hd])
        outs.append((o / jnp.sum(e, axis=-1, keepdims=True)).astype(BF16))
    xo = jnp.concatenate(outs, axis=-1)
    h_ref[...] = h1 + _dot(xo, wxo_ref[...])


def _mix_cross(x2, o_a, y_rg, gates, w_pa, w_pr, w_out, g_cross, w_xq, mkv, w_xo, tm, seq):
    T, D = x2.shape
    M = mkv.shape[1]
    row = lambda n: pl.BlockSpec((tm, n), lambda i: (i, 0))
    steps_per_batch = seq // tm
    return pl.pallas_call(
        _mix_cross_kernel,
        grid=(T // tm,),
        in_specs=[row(D), row(o_a.shape[1]), row(y_rg.shape[1]), row(gates.shape[1]),
                  _const_spec(w_pa.shape), _const_spec(w_pr.shape), _const_spec(w_out.shape),
                  _const_spec(g_cross.shape), _const_spec(w_xq.shape),
                  pl.BlockSpec((None, M, 2 * D), lambda i: (i // steps_per_batch, 0, 0)),
                  _const_spec(w_xo.shape)],
        out_specs=row(D),
        out_shape=jax.ShapeDtypeStruct((T, D), F32),
        compiler_params=_params(("parallel",)),
        name="mix_cross",
    )(x2, o_a, y_rg, gates, w_pa, w_pr, w_out, g_cross, w_xq, mkv, w_xo)


def _ffn_kernel(h_ref, gf_ref, wu_ref, wc_ref, bc_ref, wd_ref, gfin_ref,
                o_ref, n_ref, act_ref, bufa_ref, bufb_ref, haloa_ref, halob_ref, *, steps_per_batch):
    tm = h_ref.shape[0]
    nch, _, fc = haloa_ref.shape
    d_ff = nch * fc
    halo = SUBLANES

    @pl.when(pl.program_id(0) % steps_per_batch == 0)
    def _():
        haloa_ref[...] = jnp.zeros(haloa_ref.shape, F32)
        halob_ref[...] = jnp.zeros(halob_ref.shape, F32)

    n_ref[...] = _rms(h_ref[...], gf_ref[...]).astype(BF16)

    def conv(buf_ref, halo_ref, col, up, j):
        buf_ref[:halo, :] = halo_ref[j]
        buf_ref[halo:, :] = up
        halo_ref[j] = up[tm - halo:, :]
        w = wc_ref[:, col:col + fc]
        out = up * w[CONV_W - 1:CONV_W, :] + bc_ref[:, col:col + fc]
        for t in range(1, CONV_W):
            out = out + buf_ref[halo - t: halo - t + tm, :] * w[CONV_W - 1 - t:CONV_W - t, :]
        return out

    n = n_ref[...]
    up_pair = lambda j: (_dot(n, wu_ref[:, j * fc:(j + 1) * fc]), _dot(n, wu_ref[:, d_ff + j * fc:d_ff + (j + 1) * fc]))
    ups = up_pair(0)
    for j in range(nch):
        ua, ub = ups
        if j + 1 < nch:
            ups = up_pair(j + 1)
        a = conv(bufa_ref.at[j % 2], haloa_ref, j * fc, ua, j)
        b = conv(bufb_ref.at[j % 2], halob_ref, d_ff + j * fc, ub, j)
        act_ref[:, j * fc:(j + 1) * fc] = (a * jax.nn.sigmoid(a) * b).astype(BF16)

    o_ref[...] = _rms(h_ref[...] + _dot(act_ref[...], wd_ref[...]), gfin_ref[...])


def _ffn_out(h2, g_ffn, w_u, w_c, b_c, w_d, g_final, tm, seq):
    T, D = h2.shape
    fc = FF_CHUNK
    nch = w_d.shape[0] // fc
    row = pl.BlockSpec((tm, D), lambda i: (i, 0))
    consts = [g_ffn, w_u, w_c, b_c, w_d, g_final]
    return pl.pallas_call(
        functools.partial(_ffn_kernel, steps_per_batch=seq // tm),
        grid=(T // tm,),
        in_specs=[row] + [_const_spec(c.shape) for c in consts],
        out_specs=row,
        out_shape=jax.ShapeDtypeStruct((T, D), F32),
        scratch_shapes=[pltpu.VMEM((tm, D), BF16), pltpu.VMEM((tm, nch * fc), BF16),
                        pltpu.VMEM((2, tm + SUBLANES, fc), F32), pltpu.VMEM((2, tm + SUBLANES, fc), F32),
                        pltpu.VMEM((nch, SUBLANES, fc), F32), pltpu.VMEM((nch, SUBLANES, fc), F32)],
        compiler_params=_params(("arbitrary",)),
        name="ffn_out",
    )(h2, *consts)


def _rope_inv(half):
    return ROPE_THETA ** (-np.arange(half, dtype=np.float32) / np.float32(half))


def _mla_column_maps():
    half = MLA_ROPE // 2
    used = MLA_NOPE + MLA_ROPE
    q_cols = np.zeros((MLA_HEADS, HEAD_PAD), np.int32)
    k_cols = np.zeros((MLA_HEADS, HEAD_PAD), np.int32)
    valid_q = np.zeros((MLA_HEADS, HEAD_PAD), bool)
    valid_k = np.zeros((MLA_HEADS, HEAD_PAD), bool)
    for h in range(MLA_HEADS):
        q_cols[h, :used] = h * used + np.arange(used)
        valid_q[h, :used] = True
        k_cols[h, :MLA_NOPE] = h * (MLA_NOPE + MLA_V) + np.arange(MLA_NOPE)
        valid_k[h, :MLA_NOPE] = True
    v_cols = np.zeros((MLA_HEADS, V_ROWS), np.int32)
    valid_v = np.zeros((MLA_HEADS, V_ROWS), bool)
    for h in range(MLA_HEADS):
        v_cols[h, :MLA_V] = h * (MLA_NOPE + MLA_V) + MLA_NOPE + np.arange(MLA_V)
        valid_v[h, :MLA_V] = True
    del half
    return (q_cols.reshape(-1), valid_q.reshape(-1), k_cols.reshape(-1), valid_k.reshape(-1),
            v_cols.reshape(-1), valid_v.reshape(-1))


def kernel(x, mem, positions, g_mix, w_in, b_gate, g_q_lat, w_uq, g_kv_lat, w_ukv, w_proj_mla, g_ret,
           w_proj_ret, w_out, g_cross, g_mem, w_xq, w_xkv, w_xo, g_ffn, w_up, w_conv, b_conv, w_down, g_final):
    B, S, D = x.shape
    T = B * S
    depth = w_in.shape[0]
    d_ff = w_down.shape[1]
    hw = RET_HEADS * RET_DK
    tm = min(512, S)
    tq = min(512, S)
    tr = min(1024, S)
    tf = min(512, S)

    half_a = MLA_ROPE // 2
    inv_pack = np.zeros((1, LANES), np.float32)
    inv_pack[0, :RET_DK // 2] = _rope_inv(RET_DK // 2)
    inv_pack[0, MLA_NOPE:MLA_NOPE + half_a] = _rope_inv(half_a)
    q_cols, valid_q, k_cols, valid_k, v_cols, valid_v = _mla_column_maps()
    v_ones = jnp.asarray(np.where(valid_v, 0.0, 1.0)[:, None], F32)

    h = x.reshape(T, D)
    pos2 = positions.reshape(T, 1)
    row2 = lambda a: a.reshape(1, -1)
    for l in range(depth):
        w_in_p = _w_in_prep(w_in[l].T)
        w_uq_p = jnp.where(valid_q[None, :], w_uq[l][:, q_cols], 0.0).astype(BF16)
        w_uk_p = jnp.where(valid_k[None, :], w_ukv[l][:, k_cols], 0.0).astype(BF16)
        w_uv = jnp.where(valid_v[None, :], w_ukv[l][:, v_cols], 0.0).T.astype(BF16)
        assert d_ff % FF_CHUNK == 0
        w_d = w_down[l].astype(BF16)

        mkv = _mem_kv(mem, row2(g_mem[l]), w_xkv[l].astype(BF16))
        q, k, vt, rq, rk, rv, rg, gates = _in_proj(
            h, pos2, row2(g_mix[l]), w_in_p, row2(b_gate[l]), row2(g_q_lat[l]), w_uq_p,
            row2(g_kv_lat[l]), w_uk_p, w_uv, v_ones, jnp.asarray(inv_pack), tm, tq)
        seq3 = lambda a: a.reshape(B, S, a.shape[-1])
        o_a = _mla_attn(seq3(q), seq3(k), vt.reshape(B, S // tq, vt.shape[1], tq), tq)
        y_rg = _retention(seq3(rq), seq3(rk), seq3(rv), seq3(rg), row2(g_ret[l]), tr)
        h = _mix_cross(h, o_a.reshape(T, -1), y_rg.reshape(T, -1), gates, w_proj_mla[l].astype(BF16),
                       w_proj_ret[l].astype(BF16), w_out[l].astype(BF16), row2(g_cross[l]),
                       w_xq[l].astype(BF16), mkv, w_xo[l].astype(BF16), tm, S)
        last = l == depth - 1
        assert last, "only a single layer stack is supported"
        h = _ffn_out(h, row2(g_ffn[l]), w_up[l].astype(BF16), w_conv[l], row2(b_conv[l]), w_d, row2(g_final), tf, S)
    return h.reshape(B, S, D)
```

```python
import functools

import numpy as np
import jax
import jax.numpy as jnp
from jax import lax
from jax.experimental import pallas as pl
from jax.experimental.pallas import tpu as pltpu

MLA_HEADS = 8
MLA_NOPE = 64
MLA_ROPE = 32
MLA_V = 64
MLA_Q_RANK = 256
MLA_KV_RANK = 128
RET_HEADS = 4
RET_DK = 128
RET_DV = 128
RET_CHUNK = 128
X_HEADS = 4
CONV_W = 3
ROPE_THETA = 10000.0
EPS = 1e-6

LANES = 128
SUBLANES = 8
HEAD_PAD = 128
V_ROWS = MLA_V + 16
LOOP_BLOCKS = 8
FF_CHUNK = 256
VMEM_LIMIT = 56 * 1024 * 1024

BF16 = jnp.bfloat16
F32 = jnp.float32


def _rms(x, g):
    return x * lax.rsqrt(jnp.mean(x * x, axis=-1, keepdims=True) + EPS) * g


def _dot(a, b):
    return jnp.dot(a, b, preferred_element_type=F32)


def _dot_nt(a, b):
    return lax.dot_general(a, b, (((1,), (1,)), ((), ())), preferred_element_type=F32)


def _dot_tn(a, b):
    return lax.dot_general(a, b, (((0,), (0,)), ((), ())), preferred_element_type=F32)


def _const_spec(shape):
    nd = len(shape)
    return pl.BlockSpec(shape, lambda *_: (0,) * nd, pipeline_mode=pl.Buffered(1))


def _params(sem):
    return pltpu.CompilerParams(dimension_semantics=sem, vmem_limit_bytes=VMEM_LIMIT)


def _mem_kv_kernel(mem_ref, g_ref, w_ref, o_ref):
    n = _rms(mem_ref[...], g_ref[...]).astype(BF16)
    o_ref[...] = _dot(n, w_ref[...]).astype(o_ref.dtype)


def _mem_kv(mem, g_mem, w_xkv):
    B, M, D = mem.shape
    N = w_xkv.shape[1]
    return pl.pallas_call(
        _mem_kv_kernel,
        grid=(B,),
        in_specs=[pl.BlockSpec((None, M, D), lambda b: (b, 0, 0)),
                  _const_spec((1, D)), _const_spec((D, N))],
        out_specs=pl.BlockSpec((None, M, N), lambda b: (b, 0, 0)),
        out_shape=jax.ShapeDtypeStruct((B, M, N), BF16),
        compiler_params=_params(("parallel",)),
        name="mem_kv",
    )(mem, g_mem, w_xkv)


def _w_in_prep_kernel(wt_ref, o_ref):
    c0 = MLA_Q_RANK + MLA_KV_RANK
    c1 = c0 + MLA_ROPE
    cols = wt_ref.shape[1]
    o_ref[:c0, :] = wt_ref[:c0, :].astype(o_ref.dtype)
    o_ref[c0:c0 + MLA_NOPE, :] = jnp.zeros((MLA_NOPE, cols), o_ref.dtype)
    o_ref[c0 + MLA_NOPE:c0 + MLA_NOPE + MLA_ROPE, :] = wt_ref[c0:c1, :].astype(o_ref.dtype)
    o_ref[c0 + MLA_NOPE + MLA_ROPE:c0 + HEAD_PAD, :] = jnp.zeros((HEAD_PAD - MLA_NOPE - MLA_ROPE, cols), o_ref.dtype)
    o_ref[c0 + HEAD_PAD:, :] = wt_ref[c1:, :].astype(o_ref.dtype)


def _w_in_prep(w_in_t, tc=256):
    N, D = w_in_t.shape
    n_out = N - MLA_ROPE + HEAD_PAD
    return pl.pallas_call(
        _w_in_prep_kernel,
        grid=(D // tc,),
        in_specs=[pl.BlockSpec((N, tc), lambda i: (0, i))],
        out_specs=pl.BlockSpec((n_out, tc), lambda i: (0, i)),
        out_shape=jax.ShapeDtypeStruct((n_out, D), BF16),
        compiler_params=_params(("parallel",)),
        name="w_in_prep",
    )(w_in_t)


def _in_proj_kernel(x_ref, pos_ref, gmix_ref, win_ref, bgate_ref,
                    gq_ref, wuq_ref, gkv_ref, wuk_ref, wuv_ref, vones_ref, inv_ref,
                    q_ref, k_ref, vt_ref, rq_ref, rk_ref, rv_ref, rg_ref, gate_ref):
    lat_w = MLA_Q_RANK + MLA_KV_RANK + HEAD_PAD
    hw = RET_HEADS * RET_DK
    u = _rms(x_ref[...], gmix_ref[...]).astype(BF16)
    lat = _dot_nt(u, win_ref[:lat_w, :])
    ret = _dot_nt(u, win_ref[lat_w:lat_w + 4 * hw, :])
    gate_pre = _dot_nt(u, win_ref[lat_w + 4 * hw:, :])

    ang = pos_ref[...].astype(F32) * inv_ref[...]
    cos_p, sin_p = jnp.cos(ang), jnp.sin(ang)
    lane = lax.broadcasted_iota(jnp.int32, ang.shape, 1)
    half_r = RET_DK // 2
    half_a = MLA_ROPE // 2
    cos_r = jnp.where(lane < half_r, cos_p, pltpu.roll(cos_p, half_r, 1))
    sin_r = jnp.where(lane < half_r, -sin_p, pltpu.roll(sin_p, half_r, 1))
    x1_a = jnp.logical_and(lane >= MLA_NOPE, lane < MLA_NOPE + half_a)
    x2_a = jnp.logical_and(lane >= MLA_NOPE + half_a, lane < MLA_NOPE + MLA_ROPE)
    cos_a = jnp.where(x1_a, cos_p, jnp.where(x2_a, pltpu.roll(cos_p, half_a, 1), 1.0))
    sin_a = jnp.where(x1_a, -sin_p, jnp.where(x2_a, pltpu.roll(sin_p, half_a, 1), 0.0))

    def rot_a(blk):
        partner = jnp.where(x1_a,
                            pltpu.roll(blk, HEAD_PAD - half_a, 1),
                            pltpu.roll(blk, half_a, 1))
        return blk * cos_a + partner * sin_a

    cq = _rms(lat[:, :MLA_Q_RANK], gq_ref[...]).astype(BF16)
    ckv = _rms(lat[:, MLA_Q_RANK:MLA_Q_RANK + MLA_KV_RANK], gkv_ref[...]).astype(BF16)
    kr = rot_a(lat[:, MLA_Q_RANK + MLA_KV_RANK:])

    qf = _dot(cq, wuq_ref[...])
    kf = _dot(ckv, wuk_ref[...])
    q_scale = (MLA_NOPE + MLA_ROPE) ** -0.5 * np.log2(np.e)
    for h in range(MLA_HEADS):
        sl = slice(h * HEAD_PAD, (h + 1) * HEAD_PAD)
        q_ref[:, sl] = (rot_a(qf[:, sl]) * q_scale).astype(q_ref.dtype)
        k_ref[:, sl] = (kf[:, sl] + kr).astype(k_ref.dtype)
    vt_ref[...] = (_dot_nt(wuv_ref[...], ckv) + vones_ref[...]).astype(vt_ref.dtype)

    k_scale = RET_DK ** -0.5
    for h in range(RET_HEADS):
        sl = slice(h * RET_DK, (h + 1) * RET_DK)
        bq = ret[:, sl]
        rq_ref[:, sl] = (bq * cos_r + pltpu.roll(bq, RET_DK // 2, 1) * sin_r).astype(rq_ref.dtype)
        bk = ret[:, hw + h * RET_DK: hw + (h + 1) * RET_DK]
        rk_ref[:, sl] = ((bk * cos_r + pltpu.roll(bk, RET_DK // 2, 1) * sin_r) * k_scale).astype(rk_ref.dtype)
    rv_ref[...] = ret[:, 2 * hw:3 * hw].astype(rv_ref.dtype)
    rg = ret[:, 3 * hw:]
    rg_ref[...] = (rg * jax.nn.sigmoid(rg)).astype(rg_ref.dtype)

    gate_ref[...] = jax.nn.sigmoid(gate_pre + bgate_ref[...]).astype(gate_ref.dtype)


def _in_proj(x2, pos2, g_mix, w_in_p, b_gate, g_q, w_uq, g_kv, w_uk, w_uv, v_ones,
             inv_pack, tm, tkv):
    T, D = x2.shape
    hw = RET_HEADS * RET_DK
    row = lambda n: pl.BlockSpec((tm, n), lambda i: (i, 0))
    consts = [g_mix, w_in_p, b_gate, g_q, w_uq, g_kv, w_uk, w_uv, v_ones, inv_pack]
    out_widths = [MLA_HEADS * HEAD_PAD, MLA_HEADS * HEAD_PAD, hw, hw, hw, hw, 2 * D]
    out_dtypes = [BF16, BF16, F32, F32, BF16, BF16, BF16]
    vw = MLA_HEADS * V_ROWS
    per = tkv // tm
    vt_spec = pl.BlockSpec((None, vw, tm), lambda i: (i // per, 0, i % per))
    vt_shape = jax.ShapeDtypeStruct((T // tkv, vw, tkv), BF16)
    out_specs = [row(n) for n in out_widths]
    out_shape = [jax.ShapeDtypeStruct((T, n), dt) for n, dt in zip(out_widths, out_dtypes)]
    return pl.pallas_call(
        _in_proj_kernel,
        grid=(T // tm,),
        in_specs=[row(D), row(1)] + [_const_spec(c.shape) for c in consts],
        out_specs=out_specs[:2] + [vt_spec] + out_specs[2:],
        out_shape=out_shape[:2] + [vt_shape] + out_shape[2:],
        compiler_params=_params(("parallel",)),
        name="in_proj",
    )(x2, pos2, *consts)


def _mla_attn_kernel(q_ref, k_ref, vt_ref, o_ref, s_ref, mblk_ref, m_ref, acc_ref, *, tq):
    i = pl.program_id(2)
    heads = q_ref.shape[1] // HEAD_PAD

    m_ref[...] = jnp.full(m_ref.shape, -jnp.inf, F32)
    acc_ref[...] = jnp.zeros(acc_ref.shape, F32)

    def scores(j, slot):
        k0 = pl.multiple_of(j * tq, tq)
        for h in range(heads):
            sl = slice(h * HEAD_PAD, (h + 1) * HEAD_PAD)
            s = _dot_nt(k_ref[pl.ds(k0, tq), sl], q_ref[:, sl])
            s_ref[slot, h] = s
            mblk_ref[slot, h] = jnp.max(s, axis=0, keepdims=True)

    def consume(j, slot, diagonal):
        if diagonal:
            return consume_diagonal(j, slot)
        for h in range(heads):
            s = s_ref[slot, h]
            m_old = m_ref[h]
            m_new = jnp.maximum(m_old, mblk_ref[slot, h])
            alpha = jnp.exp2(m_old - m_new)
            p = jnp.exp2(s - m_new).astype(BF16)
            vt = vt_ref[j, h * V_ROWS:(h + 1) * V_ROWS, :]
            acc_ref[h] = alpha * acc_ref[h] + _dot(vt, p)
            m_ref[h] = m_new

    def consume_diagonal(j, slot):
        hq = tq // 2
        tri = (lax.broadcasted_iota(jnp.int32, (hq, hq), 0) <= lax.broadcasted_iota(jnp.int32, (hq, hq), 1))
        for h in range(heads):
            top_l = jnp.where(tri, s_ref[slot, h, :hq, :hq], -jnp.inf)
            top_r = s_ref[slot, h, :hq, hq:]
            bot_r = jnp.where(tri, s_ref[slot, h, hq:, hq:], -jnp.inf)
            m_blk = jnp.concatenate(
                [jnp.max(top_l, axis=0, keepdims=True),
                 jnp.maximum(jnp.max(top_r, axis=0, keepdims=True), jnp.max(bot_r, axis=0, keepdims=True))], axis=1)
            m_old = m_ref[h]
            m_new = jnp.maximum(m_old, m_blk)
            alpha = jnp.exp2(m_old - m_new)
            p_top = jnp.exp2(jnp.concatenate([top_l, top_r], axis=1) - m_new).astype(BF16)
            p_bot = jnp.exp2(bot_r - m_new[:, hq:]).astype(BF16)
            vt = vt_ref[j, h * V_ROWS:(h + 1) * V_ROWS, :]
            pv = _dot(vt[:, :hq], p_top)
            pv = jnp.concatenate([pv[:, :hq], pv[:, hq:] + _dot(vt[:, hq:], p_bot)], axis=1)
            acc_ref[h] = alpha * acc_ref[h] + pv
            m_ref[h] = m_new

    scores(0, 0)

    def run(first, count, with_diagonal):
        for d in range(count):
            scores(first + d + 1, (d + 1) % 2)
            consume(first + d, d % 2, False)
        if with_diagonal:
            consume(first + count, count % 2, True)

    def body(t, carry):
        run(LOOP_BLOCKS * t, LOOP_BLOCKS, False)
        return carry

    lax.fori_loop(0, i // LOOP_BLOCKS, body, 0)
    for r in range(LOOP_BLOCKS):
        @pl.when(i % LOOP_BLOCKS == r)
        def _(r=r):
            run(i - r, r, True)

    out_t = jnp.concatenate([acc_ref[h, :MLA_V] / acc_ref[h, MLA_V:MLA_V + 1] for h in range(heads)], axis=0)
    o_ref[...] = out_t.T.astype(o_ref.dtype)


def _mla_attn(q, k, vt, tq, heads_per_step=2):
    B, S, _ = q.shape
    groups = MLA_HEADS // heads_per_step
    qw = heads_per_step * HEAD_PAD
    return pl.pallas_call(
        functools.partial(_mla_attn_kernel, tq=tq),
        grid=(B, groups, S // tq),
        in_specs=[pl.BlockSpec((None, tq, qw), lambda b, g, i: (b, i, g)),
                  pl.BlockSpec((None, S, qw), lambda b, g, i: (b, 0, g)),
                  pl.BlockSpec((None, S // tq, heads_per_step * V_ROWS, tq), lambda b, g, i: (b, 0, g, 0))],
        out_specs=pl.BlockSpec((None, tq, heads_per_step * MLA_V), lambda b, g, i: (b, i, g)),
        out_shape=jax.ShapeDtypeStruct((B, S, MLA_HEADS * MLA_V), BF16),
        scratch_shapes=[pltpu.VMEM((2, heads_per_step, tq, tq), F32),
                        pltpu.VMEM((2, heads_per_step, 1, tq), F32),
                        pltpu.VMEM((heads_per_step, 1, tq), F32),
                        pltpu.VMEM((heads_per_step, V_ROWS, tq), F32)],
        compiler_params=_params(("parallel", "parallel", "parallel")),
        name="mla_attn",
    )(q, k, vt)


def _retention_tables():
    C = RET_CHUNK
    log_g = np.log1p(-np.exp2(-5.0 - np.arange(RET_HEADS, dtype=np.float64)))
    idx = np.arange(C, dtype=np.float64)
    rel = idx[:, None] - idx[None, :]
    dmask = np.where(rel >= 0, np.exp(log_g[:, None, None] * np.maximum(rel, 0.0)), 0.0)
    zeta = np.exp(log_g[:, None] * (C - 1.0 - idx)[None, :])
    xi = np.exp(log_g[:, None] * (idx + 1.0)[None, :])
    decay = np.exp(log_g * C)
    f = lambda a: jnp.asarray(a, F32)
    return f(dmask), f(zeta[:, :, None]), f(xi[:, :, None]), f(np.broadcast_to(decay[:, None, None], (RET_HEADS, 1, LANES)))


def _retention_kernel(q_ref, k_ref, v_ref, rg_ref, gret_ref, dmask_ref, zeta_ref, xi_ref, decay_ref,
                      o_ref, state_ref, *, chunks):
    @pl.when(pl.program_id(1) == 0)
    def _():
        state_ref[...] = jnp.zeros(state_ref.shape, F32)

    C = RET_CHUNK
    for h in range(RET_HEADS):
        sl = slice(h * RET_DK, (h + 1) * RET_DK)
        dmask, zeta, xi = dmask_ref[h], zeta_ref[h], xi_ref[h]
        decay = decay_ref[h][:, :RET_DV]
        inner, kv, qx = [], [], []
        for c in range(chunks):
            rows = slice(c * C, (c + 1) * C)
            q, k, v = q_ref[rows, sl], k_ref[rows, sl], v_ref[rows, sl]
            s = (_dot_nt(q.astype(BF16), k.astype(BF16)) * dmask).astype(BF16)
            inner.append(_dot(s, v))
            kv.append(_dot_tn((k * zeta).astype(BF16), v))
            qx.append((q * xi).astype(BF16))
        state = state_ref[h]
        for c in range(chunks):
            rows = slice(c * C, (c + 1) * C)
            y = inner[c] + _dot(qx[c], state.astype(BF16))
            state = state * decay + kv[c]
            mu = jnp.mean(y, axis=-1, keepdims=True)
            yc = y - mu
            var = jnp.mean(yc * yc, axis=-1, keepdims=True)
            yn = yc * lax.rsqrt(var + EPS) * gret_ref[:, sl]
            o_ref[rows, sl] = (rg_ref[rows, sl].astype(F32) * yn).astype(o_ref.dtype)
        state_ref[h] = state


def _retention(rq, rk, rv, rg, g_ret, tr):
    B, S, W = rq.shape
    tables = _retention_tables()
    blk = pl.BlockSpec((None, tr, W), lambda b, i: (b, i, 0))
    return pl.pallas_call(
        functools.partial(_retention_kernel, chunks=tr // RET_CHUNK),
        grid=(B, S // tr),
        in_specs=[blk, blk, blk, blk, _const_spec(g_ret.shape)] + [_const_spec(t.shape) for t in tables],
        out_specs=blk,
        out_shape=jax.ShapeDtypeStruct((B, S, W), BF16),
        scratch_shapes=[pltpu.VMEM((RET_HEADS, RET_DK, RET_DV), F32)],
        compiler_params=_params(("parallel", "arbitrary")),
        name="retention",
    )(rq, rk, rv, rg, g_ret, *tables)


def _mix_cross_kernel(x_ref, oa_ref, yr_ref, gate_ref, wpa_ref, wpr_ref, wout_ref, gc_ref, wxq_ref,
                      mkv_ref, wxo_ref, h_ref):
    D = x_ref.shape[1]
    hd = D // X_HEADS
    y_a = _dot(oa_ref[...], wpa_ref[...])
    y_r = _dot(yr_ref[...], wpr_ref[...])
    merged = gate_ref[:, :D].astype(F32) * y_a + gate_ref[:, D:].astype(F32) * y_r
    h1 = x_ref[...] + _dot(merged.astype(BF16), wout_ref[...])

    xq = _dot(_rms(h1, gc_ref[...]).astype(BF16), wxq_ref[...]).astype(BF16)
    scale = hd ** -0.5
    outs = []
    for h in range(X_HEADS):
        sl = slice(h * hd, (h + 1) * hd)
        s = _dot_nt(xq[:, sl], mkv_ref[:, sl]) * scale
        e = jnp.exp(s - jnp.max(s, axis=-1, keepdims=True))
        o = _dot(e.astype(BF16), mkv_ref[:, D + h * hd: D + (h + 1) * hd])
        outs.append((o / jnp.sum(e, axis=-1, keepdims=True)).astype(BF16))
    xo = jnp.concatenate(outs, axis=-1)
    h_ref[...] = h1 + _dot(xo, wxo_ref[...])


def _mix_cross(x2, o_a, y_rg, gates, w_pa, w_pr, w_out, g_cross, w_xq, mkv, w_xo, tm, seq):
    T, D = x2.shape
    M = mkv.shape[1]
    row = lambda n: pl.BlockSpec((tm, n), lambda i: (i, 0))
    steps_per_batch = seq // tm
    return pl.pallas_call(
        _mix_cross_kernel,
        grid=(T // tm,),
        in_specs=[row(D), row(o_a.shape[1]), row(y_rg.shape[1]), row(gates.shape[1]),
                  _const_spec(w_pa.shape), _const_spec(w_pr.shape), _const_spec(w_out.shape),
                  _const_spec(g_cross.shape), _const_spec(w_xq.shape),
                  pl.BlockSpec((None, M, 2 * D), lambda i: (i // steps_per_batch, 0, 0)),
                  _const_spec(w_xo.shape)],
        out_specs=row(D),
        out_shape=jax.ShapeDtypeStruct((T, D), F32),
        compiler_params=_params(("parallel",)),
        name="mix_cross",
    )(x2, o_a, y_rg, gates, w_pa, w_pr, w_out, g_cross, w_xq, mkv, w_xo)


def _ffn_kernel(h_ref, gf_ref, wu_ref, wc_ref, bc_ref, wd_ref, gfin_ref,
                o_ref, n_ref, act_ref, bufa_ref, bufb_ref, haloa_ref, halob_ref, *, steps_per_batch):
    tm = h_ref.shape[0]
    nch, _, fc = haloa_ref.shape
    d_ff = nch * fc
    halo = SUBLANES

    @pl.when(pl.program_id(0) % steps_per_batch == 0)
    def _():
        haloa_ref[...] = jnp.zeros(haloa_ref.shape, F32)
        halob_ref[...] = jnp.zeros(halob_ref.shape, F32)

    n_ref[...] = _rms(h_ref[...], gf_ref[...]).astype(BF16)

    def conv(buf_ref, halo_ref, col, up, j):
        buf_ref[:halo, :] = halo_ref[j]
        buf_ref[halo:, :] = up
        halo_ref[j] = up[tm - halo:, :]
        w = wc_ref[:, col:col + fc]
        out = up * w[CONV_W - 1:CONV_W, :] + bc_ref[:, col:col + fc]
        for t in range(1, CONV_W):
            out = out + buf_ref[halo - t: halo - t + tm, :] * w[CONV_W - 1 - t:CONV_W - t, :]
        return out

    n = n_ref[...]
    up_pair = lambda j: (_dot(n, wu_ref[:, j * fc:(j + 1) * fc]), _dot(n, wu_ref[:, d_ff + j * fc:d_ff + (j + 1) * fc]))
    ups = up_pair(0)
    for j in range(nch):
        ua, ub = ups
        if j + 1 < nch:
            ups = up_pair(j + 1)
        a = conv(bufa_ref.at[j % 2], haloa_ref, j * fc, ua, j)
        b = conv(bufb_ref.at[j % 2], halob_ref, d_ff + j * fc, ub, j)
        act_ref[:, j * fc:(j + 1) * fc] = (a * jax.nn.sigmoid(a) * b).astype(BF16)

    o_ref[...] = _rms(h_ref[...] + _dot(act_ref[...], wd_ref[...]), gfin_ref[...])


def _ffn_out(h2, g_ffn, w_u, w_c, b_c, w_d, g_final, tm, seq):
    T, D = h2.shape
    fc = FF_CHUNK
    nch = w_d.shape[0] // fc
    row = pl.BlockSpec((tm, D), lambda i: (i, 0))
    consts = [g_ffn, w_u, w_c, b_c, w_d, g_final]
    return pl.pallas_call(
        functools.partial(_ffn_kernel, steps_per_batch=seq // tm),
        grid=(T // tm,),
        in_specs=[row] + [_const_spec(c.shape) for c in consts],
        out_specs=row,
        out_shape=jax.ShapeDtypeStruct((T, D), F32),
        scratch_shapes=[pltpu.VMEM((tm, D), BF16), pltpu.VMEM((tm, nch * fc), BF16),
                        pltpu.VMEM((2, tm + SUBLANES, fc), F32), pltpu.VMEM((2, tm + SUBLANES, fc), F32),
                        pltpu.VMEM((nch, SUBLANES, fc), F32), pltpu.VMEM((nch, SUBLANES, fc), F32)],
        compiler_params=_params(("arbitrary",)),
        name="ffn_out",
    )(h2, *consts)


def _rope_inv(half):
    return ROPE_THETA ** (-np.arange(half, dtype=np.float32) / np.float32(half))


def _mla_column_maps():
    half = MLA_ROPE // 2
    used = MLA_NOPE + MLA_ROPE
    q_cols = np.zeros((MLA_HEADS, HEAD_PAD), np.int32)
    k_cols = np.zeros((MLA_HEADS, HEAD_PAD), np.int32)
    valid_q = np.zeros((MLA_HEADS, HEAD_PAD), bool)
    valid_k = np.zeros((MLA_HEADS, HEAD_PAD), bool)
    for h in range(MLA_HEADS):
        q_cols[h, :used] = h * used + np.arange(used)
        valid_q[h, :used] = True
        k_cols[h, :MLA_NOPE] = h * (MLA_NOPE + MLA_V) + np.arange(MLA_NOPE)
        valid_k[h, :MLA_NOPE] = True
    v_cols = np.zeros((MLA_HEADS, V_ROWS), np.int32)
    valid_v = np.zeros((MLA_HEADS, V_ROWS), bool)
    for h in range(MLA_HEADS):
        v_cols[h, :MLA_V] = h * (MLA_NOPE + MLA_V) + MLA_NOPE + np.arange(MLA_V)
        valid_v[h, :MLA_V] = True
    del half
    return (q_cols.reshape(-1), valid_q.reshape(-1), k_cols.reshape(-1), valid_k.reshape(-1),
            v_cols.reshape(-1), valid_v.reshape(-1))


def kernel(x, mem, positions, g_mix, w_in, b_gate, g_q_lat, w_uq, g_kv_lat, w_ukv, w_proj_mla, g_ret,
           w_proj_ret, w_out, g_cross, g_mem, w_xq, w_xkv, w_xo, g_ffn, w_up, w_conv, b_conv, w_down, g_final):
    B, S, D = x.shape
    T = B * S
    depth = w_in.shape[0]
    d_ff = w_down.shape[1]
    hw = RET_HEADS * RET_DK
    tm = min(512, S)
    tq = min(512, S)
    tr = min(512, S)
    tf = min(512, S)

    half_a = MLA_ROPE // 2
    inv_pack = np.zeros((1, LANES), np.float32)
    inv_pack[0, :RET_DK // 2] = _rope_inv(RET_DK // 2)
    inv_pack[0, MLA_NOPE:MLA_NOPE + half_a] = _rope_inv(half_a)
    q_cols, valid_q, k_cols, valid_k, v_cols, valid_v = _mla_column_maps()
    v_ones = jnp.asarray(np.where(valid_v, 0.0, 1.0)[:, None], F32)

    h = x.reshape(T, D)
    pos2 = positions.reshape(T, 1)
    row2 = lambda a: a.reshape(1, -1)
    for l in range(depth):
        w_in_p = _w_in_prep(w_in[l].T)
        w_uq_p = jnp.where(valid_q[None, :], w_uq[l][:, q_cols], 0.0).astype(BF16)
        w_uk_p = jnp.where(valid_k[None, :], w_ukv[l][:, k_cols], 0.0).astype(BF16)
        w_uv = jnp.where(valid_v[None, :], w_ukv[l][:, v_cols], 0.0).T.astype(BF16)
        assert d_ff % FF_CHUNK == 0
        w_d = w_down[l].astype(BF16)

        mkv = _mem_kv(mem, row2(g_mem[l]), w_xkv[l].astype(BF16))
        q, k, vt, rq, rk, rv, rg, gates = _in_proj(
            h, pos2, row2(g_mix[l]), w_in_p, row2(b_gate[l]), row2(g_q_lat[l]), w_uq_p,
            row2(g_kv_lat[l]), w_uk_p, w_uv, v_ones, jnp.asarray(inv_pack), tm, tq)
        seq3 = lambda a: a.reshape(B, S, a.shape[-1])
        o_a = _mla_attn(seq3(q), seq3(k), vt.reshape(B, S // tq, vt.shape[1], tq), tq)
        y_rg = _retention(seq3(rq), seq3(rk), seq3(rv), seq3(rg), row2(g_ret[l]), tr)
        h = _mix_cross(h, o_a.reshape(T, -1), y_rg.reshape(T, -1), gates, w_proj_mla[l].astype(BF16),
                       w_proj_ret[l].astype(BF16), w_out[l].astype(BF16), row2(g_cross[l]),
                       w_xq[l].astype(BF16), mkv, w_xo[l].astype(BF16), tm, S)
        last = l == depth - 1
        assert last, "only a single layer stack is supported"
        h = _ffn_out(h, row2(g_ffn[l]), w_up[l].astype(BF16), w_conv[l], row2(b_conv[l]), w_d, row2(g_final), tf, S)
    return h.reshape(B, S, D)
```

```python
import functools

import numpy as np
import jax
import jax.numpy as jnp
from jax import lax
from jax.experimental import pallas as pl
from jax.experimental.pallas import tpu as pltpu

MLA_HEADS = 8
MLA_NOPE = 64
MLA_ROPE = 32
MLA_V = 64
MLA_Q_RANK = 256
MLA_KV_RANK = 128
RET_HEADS = 4
RET_DK = 128
RET_DV = 128
RET_CHUNK = 128
X_HEADS = 4
CONV_W = 3
ROPE_THETA = 10000.0
EPS = 1e-6

LANES = 128
SUBLANES = 8
HEAD_PAD = 128
V_ROWS = MLA_V + 16
LOOP_BLOCKS = 8
FF_CHUNK = 256
VMEM_LIMIT = 56 * 1024 * 1024

BF16 = jnp.bfloat16
F32 = jnp.float32


def _rms(x, g):
    return x * lax.rsqrt(jnp.mean(x * x, axis=-1, keepdims=True) + EPS) * g


def _dot(a, b):
    return jnp.dot(a, b, preferred_element_type=F32)


def _dot_nt(a, b):
    return lax.dot_general(a, b, (((1,), (1,)), ((), ())), preferred_element_type=F32)


def _dot_tn(a, b):
    return lax.dot_general(a, b, (((0,), (0,)), ((), ())), preferred_element_type=F32)


def _const_spec(shape):
    nd = len(shape)
    return pl.BlockSpec(shape, lambda *_: (0,) * nd, pipeline_mode=pl.Buffered(1))


def _params(sem):
    return pltpu.CompilerParams(dimension_semantics=sem, vmem_limit_bytes=VMEM_LIMIT)


def _mem_kv_kernel(mem_ref, g_ref, w_ref, o_ref):
    n = _rms(mem_ref[...], g_ref[...]).astype(BF16)
    o_ref[...] = _dot(n, w_ref[...]).astype(o_ref.dtype)


def _mem_kv(mem, g_mem, w_xkv):
    B, M, D = mem.shape
    N = w_xkv.shape[1]
    return pl.pallas_call(
        _mem_kv_kernel,
        grid=(B,),
        in_specs=[pl.BlockSpec((None, M, D), lambda b: (b, 0, 0)),
                  _const_spec((1, D)), _const_spec((D, N))],
        out_specs=pl.BlockSpec((None, M, N), lambda b: (b, 0, 0)),
        out_shape=jax.ShapeDtypeStruct((B, M, N), BF16),
        compiler_params=_params(("parallel",)),
        name="mem_kv",
    )(mem, g_mem, w_xkv)


def _w_in_prep_kernel(wt_ref, o_ref):
    c0 = MLA_Q_RANK + MLA_KV_RANK
    c1 = c0 + MLA_ROPE
    cols = wt_ref.shape[1]
    o_ref[:c0, :] = wt_ref[:c0, :].astype(o_ref.dtype)
    o_ref[c0:c0 + MLA_NOPE, :] = jnp.zeros((MLA_NOPE, cols), o_ref.dtype)
    o_ref[c0 + MLA_NOPE:c0 + MLA_NOPE + MLA_ROPE, :] = wt_ref[c0:c1, :].astype(o_ref.dtype)
    o_ref[c0 + MLA_NOPE + MLA_ROPE:c0 + HEAD_PAD, :] = jnp.zeros((HEAD_PAD - MLA_NOPE - MLA_ROPE, cols), o_ref.dtype)
    o_ref[c0 + HEAD_PAD:, :] = wt_ref[c1:, :].astype(o_ref.dtype)


def _w_in_prep(w_in_t, tc=256):
    N, D = w_in_t.shape
    n_out = N - MLA_ROPE + HEAD_PAD
    return pl.pallas_call(
        _w_in_prep_kernel,
        grid=(D // tc,),
        in_specs=[pl.BlockSpec((N, tc), lambda i: (0, i))],
        out_specs=pl.BlockSpec((n_out, tc), lambda i: (0, i)),
        out_shape=jax.ShapeDtypeStruct((n_out, D), BF16),
        compiler_params=_params(("parallel",)),
        name="w_in_prep",
    )(w_in_t)


def _in_proj_kernel(x_ref, pos_ref, gmix_ref, win_ref, bgate_ref,
                    gq_ref, wuq_ref, gkv_ref, wuk_ref, wuv_ref, vones_ref, inv_ref,
                    q_ref, k_ref, vt_ref, rq_ref, rk_ref, rv_ref, rg_ref, gate_ref):
    lat_w = MLA_Q_RANK + MLA_KV_RANK + HEAD_PAD
    hw = RET_HEADS * RET_DK
    u = _rms(x_ref[...], gmix_ref[...]).astype(BF16)
    lat = _dot_nt(u, win_ref[:lat_w, :])
    ret = _dot_nt(u, win_ref[lat_w:lat_w + 4 * hw, :])
    gate_pre = _dot_nt(u, win_ref[lat_w + 4 * hw:, :])

    ang = pos_ref[...].astype(F32) * inv_ref[...]
    cos_p, sin_p = jnp.cos(ang), jnp.sin(ang)
    lane = lax.broadcasted_iota(jnp.int32, ang.shape, 1)
    half_r = RET_DK // 2
    half_a = MLA_ROPE // 2
    cos_r = jnp.where(lane < half_r, cos_p, pltpu.roll(cos_p, half_r, 1))
    sin_r = jnp.where(lane < half_r, -sin_p, pltpu.roll(sin_p, half_r, 1))
    x1_a = jnp.logical_and(lane >= MLA_NOPE, lane < MLA_NOPE + half_a)
    x2_a = jnp.logical_and(lane >= MLA_NOPE + half_a, lane < MLA_NOPE + MLA_ROPE)
    cos_a = jnp.where(x1_a, cos_p, jnp.where(x2_a, pltpu.roll(cos_p, half_a, 1), 1.0))
    sin_a = jnp.where(x1_a, -sin_p, jnp.where(x2_a, pltpu.roll(sin_p, half_a, 1), 0.0))

    def rot_a(blk):
        partner = jnp.where(x1_a,
                            pltpu.roll(blk, HEAD_PAD - half_a, 1),
                            pltpu.roll(blk, half_a, 1))
        return blk * cos_a + partner * sin_a

    cq = _rms(lat[:, :MLA_Q_RANK], gq_ref[...]).astype(BF16)
    ckv = _rms(lat[:, MLA_Q_RANK:MLA_Q_RANK + MLA_KV_RANK], gkv_ref[...]).astype(BF16)
    kr = rot_a(lat[:, MLA_Q_RANK + MLA_KV_RANK:])

    qf = _dot(cq, wuq_ref[...])
    kf = _dot(ckv, wuk_ref[...])
    q_scale = (MLA_NOPE + MLA_ROPE) ** -0.5 * np.log2(np.e)
    for h in range(MLA_HEADS):
        sl = slice(h * HEAD_PAD, (h + 1) * HEAD_PAD)
        q_ref[:, sl] = (rot_a(qf[:, sl]) * q_scale).astype(q_ref.dtype)
        k_ref[:, sl] = (kf[:, sl] + kr).astype(k_ref.dtype)
    vt_ref[...] = (_dot_nt(wuv_ref[...], ckv) + vones_ref[...]).astype(vt_ref.dtype)

    k_scale = RET_DK ** -0.5
    for h in range(RET_HEADS):
        sl = slice(h * RET_DK, (h + 1) * RET_DK)
        bq = ret[:, sl]
        rq_ref[:, sl] = (bq * cos_r + pltpu.roll(bq, RET_DK // 2, 1) * sin_r).astype(rq_ref.dtype)
        bk = ret[:, hw + h * RET_DK: hw + (h + 1) * RET_DK]
        rk_ref[:, sl] = ((bk * cos_r + pltpu.roll(bk, RET_DK // 2, 1) * sin_r) * k_scale).astype(rk_ref.dtype)
    rv_ref[...] = ret[:, 2 * hw:3 * hw].astype(rv_ref.dtype)
    rg = ret[:, 3 * hw:]
    rg_ref[...] = (rg * jax.nn.sigmoid(rg)).astype(rg_ref.dtype)

    gate_ref[...] = jax.nn.sigmoid(gate_pre + bgate_ref[...]).astype(gate_ref.dtype)


def _in_proj(x2, pos2, g_mix, w_in_p, b_gate, g_q, w_uq, g_kv, w_uk, w_uv, v_ones,
             inv_pack, tm, tkv):
    T, D = x2.shape
    hw = RET_HEADS * RET_DK
    row = lambda n: pl.BlockSpec((tm, n), lambda i: (i, 0))
    consts = [g_mix, w_in_p, b_gate, g_q, w_uq, g_kv, w_uk, w_uv, v_ones, inv_pack]
    out_widths = [MLA_HEADS * HEAD_PAD, MLA_HEADS * HEAD_PAD, hw, hw, hw, hw, 2 * D]
    out_dtypes = [BF16, BF16, F32, F32, BF16, BF16, BF16]
    vw = MLA_HEADS * V_ROWS
    per = tkv // tm
    vt_spec = pl.BlockSpec((None, vw, tm), lambda i: (i // per, 0, i % per))
    vt_shape = jax.ShapeDtypeStruct((T // tkv, vw, tkv), BF16)
    out_specs = [row(n) for n in out_widths]
    out_shape = [jax.ShapeDtypeStruct((T, n), dt) for n, dt in zip(out_widths, out_dtypes)]
    return pl.pallas_call(
        _in_proj_kernel,
        grid=(T // tm,),
        in_specs=[row(D), row(1)] + [_const_spec(c.shape) for c in consts],
        out_specs=out_specs[:2] + [vt_spec] + out_specs[2:],
        out_shape=out_shape[:2] + [vt_shape] + out_shape[2:],
        compiler_params=_params(("parallel",)),
        name="in_proj",
    )(x2, pos2, *consts)


def _mla_attn_kernel(q_ref, k_ref, vt_ref, o_ref, s_ref, mblk_ref, m_ref, acc_ref, *, tq):
    i = pl.program_id(2)
    heads = q_ref.shape[1] // HEAD_PAD

    m_ref[...] = jnp.full(m_ref.shape, -jnp.inf, F32)
    acc_ref[...] = jnp.zeros(acc_ref.shape, F32)

    def scores(j, slot):
        k0 = pl.multiple_of(j * tq, tq)
        for h in range(heads):
            sl = slice(h * HEAD_PAD, (h + 1) * HEAD_PAD)
            s = _dot_nt(k_ref[pl.ds(k0, tq), sl], q_ref[:, sl])
            s_ref[slot, h] = s
            mblk_ref[slot, h] = jnp.max(s, axis=0, keepdims=True)

    def consume(j, slot, diagonal):
        if diagonal:
            return consume_diagonal(j, slot)
        for h in range(heads):
            s = s_ref[slot, h]
            m_old = m_ref[h]
            m_new = jnp.maximum(m_old, mblk_ref[slot, h])
            alpha = jnp.exp2(m_old - m_new)
            p = jnp.exp2(s - m_new).astype(BF16)
            vt = vt_ref[j, h * V_ROWS:(h + 1) * V_ROWS, :]
            acc_ref[h] = alpha * acc_ref[h] + _dot(vt, p)
            m_ref[h] = m_new

    def consume_diagonal(j, slot):
        hq = tq // 2
        tri = (lax.broadcasted_iota(jnp.int32, (hq, hq), 0) <= lax.broadcasted_iota(jnp.int32, (hq, hq), 1))
        for h in range(heads):
            top_l = jnp.where(tri, s_ref[slot, h, :hq, :hq], -jnp.inf)
            top_r = s_ref[slot, h, :hq, hq:]
            bot_r = jnp.where(tri, s_ref[slot, h, hq:, hq:], -jnp.inf)
            m_blk = jnp.concatenate(
                [jnp.max(top_l, axis=0, keepdims=True),
                 jnp.maximum(jnp.max(top_r, axis=0, keepdims=True), jnp.max(bot_r, axis=0, keepdims=True))], axis=1)
            m_old = m_ref[h]
            m_new = jnp.maximum(m_old, m_blk)
            alpha = jnp.exp2(m_old - m_new)
            p_top = jnp.exp2(jnp.concatenate([top_l, top_r], axis=1) - m_new).astype(BF16)
            p_bot = jnp.exp2(bot_r - m_new[:, hq:]).astype(BF16)
            vt = vt_ref[j, h * V_ROWS:(h + 1) * V_ROWS, :]
            pv = _dot(vt[:, :hq], p_top)
            pv = jnp.concatenate([pv[:, :hq], pv[:, hq:] + _dot(vt[:, hq:], p_bot)], axis=1)
            acc_ref[h] = alpha * acc_ref[h] + pv
            m_ref[h] = m_new

    scores(0, 0)

    def run(first, count, with_diagonal):
        for d in range(count):
            scores(first + d + 1, (d + 1) % 2)
            consume(first + d, d % 2, False)
        if with_diagonal:
            consume(first + count, count % 2, True)

    def body(t, carry):
        run(LOOP_BLOCKS * t, LOOP_BLOCKS, False)
        return carry

    lax.fori_loop(0, i // LOOP_BLOCKS, body, 0)
    for r in range(LOOP_BLOCKS):
        @pl.when(i % LOOP_BLOCKS == r)
        def _(r=r):
            run(i - r, r, True)

    out_t = jnp.concatenate([acc_ref[h, :MLA_V] / acc_ref[h, MLA_V:MLA_V + 1] for h in range(heads)], axis=0)
    o_ref[...] = out_t.T.astype(o_ref.dtype)


def _mla_attn(q, k, vt, tq, heads_per_step=2):
    B, S, _ = q.shape
    groups = MLA_HEADS // heads_per_step
    qw = heads_per_step * HEAD_PAD
    return pl.pallas_call(
        functools.partial(_mla_attn_kernel, tq=tq),
        grid=(B, groups, S // tq),
        in_specs=[pl.BlockSpec((None, tq, qw), lambda b, g, i: (b, i, g)),
                  pl.BlockSpec((None, S, qw), lambda b, g, i: (b, 0, g)),
                  pl.BlockSpec((None, S // tq, heads_per_step * V_ROWS, tq), lambda b, g, i: (b, 0, g, 0))],
        out_specs=pl.BlockSpec((None, tq, heads_per_step * MLA_V), lambda b, g, i: (b, i, g)),
        out_shape=jax.ShapeDtypeStruct((B, S, MLA_HEADS * MLA_V), BF16),
        scratch_shapes=[pltpu.VMEM((2, heads_per_step, tq, tq), F32),
                        pltpu.VMEM((2, heads_per_step, 1, tq), F32),
                        pltpu.VMEM((heads_per_step, 1, tq), F32),
                        pltpu.VMEM((heads_per_step, V_ROWS, tq), F32)],
        compiler_params=_params(("parallel", "parallel", "parallel")),
        name="mla_attn",
    )(q, k, vt)


def _retention_tables():
    C = RET_CHUNK
    log_g = np.log1p(-np.exp2(-5.0 - np.arange(RET_HEADS, dtype=np.float64)))
    idx = np.arange(C, dtype=np.float64)
    rel = idx[:, None] - idx[None, :]
    dmask = np.where(rel >= 0, np.exp(log_g[:, None, None] * np.maximum(rel, 0.0)), 0.0)
    zeta = np.exp(log_g[:, None] * (C - 1.0 - idx)[None, :])
    xi = np.exp(log_g[:, None] * (idx + 1.0)[None, :])
    decay = np.exp(log_g * C)
    f = lambda a: jnp.asarray(a, F32)
    return f(dmask), f(zeta[:, :, None]), f(xi[:, :, None]), f(np.broadcast_to(decay[:, None, None], (RET_HEADS, 1, LANES)))


def _retention_kernel(q_ref, k_ref, v_ref, rg_ref, gret_ref, dmask_ref, zeta_ref, xi_ref, decay_ref,
                      o_ref, state_ref, *, chunks):
    @pl.when(pl.program_id(1) == 0)
    def _():
        state_ref[...] = jnp.zeros(state_ref.shape, F32)

    C = RET_CHUNK
    for h in range(RET_HEADS):
        sl = slice(h * RET_DK, (h + 1) * RET_DK)
        dmask, zeta, xi = dmask_ref[h], zeta_ref[h], xi_ref[h]
        decay = decay_ref[h][:, :RET_DV]
        inner, kv, qx = [], [], []
        for c in range(chunks):
            rows = slice(c * C, (c + 1) * C)
            q, k, v = q_ref[rows, sl], k_ref[rows, sl], v_ref[rows, sl]
            s = (_dot_nt(q.astype(BF16), k.astype(BF16)) * dmask).astype(BF16)
            inner.append(_dot(s, v))
            kv.append(_dot_tn((k * zeta).astype(BF16), v))
            qx.append((q * xi).astype(BF16))
        state = state_ref[h]
        for c in range(chunks):
            rows = slice(c * C, (c + 1) * C)
            y = inner[c] + _dot(qx[c], state.astype(BF16))
            state = state * decay + kv[c]
            mu = jnp.mean(y, axis=-1, keepdims=True)
            yc = y - mu
            var = jnp.mean(yc * yc, axis=-1, keepdims=True)
            yn = yc * lax.rsqrt(var + EPS) * gret_ref[:, sl]
            o_ref[rows, sl] = (rg_ref[rows, sl].astype(F32) * yn).astype(o_ref.dtype)
        state_ref[h] = state


def _retention(rq, rk, rv, rg, g_ret, tr):
    B, S, W = rq.shape
    tables = _retention_tables()
    blk = pl.BlockSpec((None, tr, W), lambda b, i: (b, i, 0))
    return pl.pallas_call(
        functools.partial(_retention_kernel, chunks=tr // RET_CHUNK),
        grid=(B, S // tr),
        in_specs=[blk, blk, blk, blk, _const_spec(g_ret.shape)] + [_const_spec(t.shape) for t in tables],
        out_specs=blk,
        out_shape=jax.ShapeDtypeStruct((B, S, W), BF16),
        scratch_shapes=[pltpu.VMEM((RET_HEADS, RET_DK, RET_DV), F32)],
        compiler_params=_params(("parallel", "arbitrary")),
        name="retention",
    )(rq, rk, rv, rg, g_ret, *tables)


def _mix_cross_kernel(x_ref, oa_ref, yr_ref, gate_ref, wpa_ref, wpr_ref, wout_ref, gc_ref, wxq_ref,
                      mkv_ref, wxo_ref, h_ref):
    D = x_ref.shape[1]
    hd = D // X_HEADS
    y_a = _dot(oa_ref[...], wpa_ref[...])
    y_r = _dot(yr_ref[...], wpr_ref[...])
    merged = gate_ref[:, :D].astype(F32) * y_a + gate_ref[:, D:].astype(F32) * y_r
    h1 = x_ref[...] + _dot(merged.astype(BF16), wout_ref[...])

    xq = _dot(_rms(h1, gc_ref[...]).astype(BF16), wxq_ref[...]).astype(BF16)
    scale = hd ** -0.5
    outs = []
    for h in range(X_HEADS):
        sl = slice(h * hd, (h + 1) * hd)
        s = _dot_nt(xq[:, sl], mkv_ref[:, sl]) * scale
        e = jnp.exp(s - jnp.max(s, axis=-1, keepdims=True))
        o = _dot(e.astype(BF16), mkv_ref[:, D + h * hd: D + (h + 1) * hd])
        outs.append((o / jnp.sum(e, axis=-1, keepdims=True)).astype(BF16))
    xo = jnp.concatenate(outs, axis=-1)
    h_ref[...] = h1 + _dot(xo, wxo_ref[...])


def _mix_cross(x2, o_a, y_rg, gates, w_pa, w_pr, w_out, g_cross, w_xq, mkv, w_xo, tm, seq):
    T, D = x2.shape
    M = mkv.shape[1]
    row = lambda n: pl.BlockSpec((tm, n), lambda i: (i, 0))
    steps_per_batch = seq // tm
    return pl.pallas_call(
        _mix_cross_kernel,
        grid=(T // tm,),
        in_specs=[row(D), row(o_a.shape[1]), row(y_rg.shape[1]), row(gates.shape[1]),
                  _const_spec(w_pa.shape), _const_spec(w_pr.shape), _const_spec(w_out.shape),
                  _const_spec(g_cross.shape), _const_spec(w_xq.shape),
                  pl.BlockSpec((None, M, 2 * D), lambda i: (i // steps_per_batch, 0, 0)),
                  _const_spec(w_xo.shape)],
        out_specs=row(D),
        out_shape=jax.ShapeDtypeStruct((T, D), F32),
        compiler_params=_params(("parallel",)),
        name="mix_cross",
    )(x2, o_a, y_rg, gates, w_pa, w_pr, w_out, g_cross, w_xq, mkv, w_xo)


def _ffn_kernel(h_ref, gf_ref, wu_ref, wc_ref, bc_ref, wd_ref, gfin_ref,
                o_ref, n_ref, act_ref, bufa_ref, bufb_ref, haloa_ref, halob_ref, *, steps_per_batch):
    tm = h_ref.shape[0]
    nch, _, fc = haloa_ref.shape
    d_ff = nch * fc
    halo = SUBLANES

    @pl.when(pl.program_id(0) % steps_per_batch == 0)
    def _():
        haloa_ref[...] = jnp.zeros(haloa_ref.shape, F32)
        halob_ref[...] = jnp.zeros(halob_ref.shape, F32)

    n_ref[...] = _rms(h_ref[...], gf_ref[...]).astype(BF16)

    def conv(buf_ref, halo_ref, col, up, j):
        buf_ref[:halo, :] = halo_ref[j]
        buf_ref[halo:, :] = up
        halo_ref[j] = up[tm - halo:, :]
        w = wc_ref[:, col:col + fc]
        out = up * w[CONV_W - 1:CONV_W, :] + bc_ref[:, col:col + fc]
        for t in range(1, CONV_W):
            out = out + buf_ref[halo - t: halo - t + tm, :] * w[CONV_W - 1 - t:CONV_W - t, :]
        return out

    n = n_ref[...]
    up_pair = lambda j: (_dot(n, wu_ref[:, j * fc:(j + 1) * fc]), _dot(n, wu_ref[:, d_ff + j * fc:d_ff + (j + 1) * fc]))
    ups = up_pair(0)
    for j in range(nch):
        ua, ub = ups
        if j + 1 < nch:
            ups = up_pair(j + 1)
        a = conv(bufa_ref.at[j % 2], haloa_ref, j * fc, ua, j)
        b = conv(bufb_ref.at[j % 2], halob_ref, d_ff + j * fc, ub, j)
        act_ref[:, j * fc:(j + 1) * fc] = (a * jax.nn.sigmoid(a) * b).astype(BF16)

    o_ref[...] = _rms(h_ref[...] + _dot(act_ref[...], wd_ref[...]), gfin_ref[...])


def _ffn_out(h2, g_ffn, w_u, w_c, b_c, w_d, g_final, tm, seq):
    T, D = h2.shape
    fc = FF_CHUNK
    nch = w_d.shape[0] // fc
    row = pl.BlockSpec((tm, D), lambda i: (i, 0))
    consts = [g_ffn, w_u, w_c, b_c, w_d, g_final]
    return pl.pallas_call(
        functools.partial(_ffn_kernel, steps_per_batch=seq // tm),
        grid=(T // tm,),
        in_specs=[row] + [_const_spec(c.shape) for c in consts],
        out_specs=row,
        out_shape=jax.ShapeDtypeStruct((T, D), F32),
        scratch_shapes=[pltpu.VMEM((tm, D), BF16), pltpu.VMEM((tm, nch * fc), BF16),
                        pltpu.VMEM((2, tm + SUBLANES, fc), F32), pltpu.VMEM((2, tm + SUBLANES, fc), F32),
                        pltpu.VMEM((nch, SUBLANES, fc), F32), pltpu.VMEM((nch, SUBLANES, fc), F32)],
        compiler_params=_params(("arbitrary",)),
        name="ffn_out",
    )(h2, *consts)


def _rope_inv(half):
    return ROPE_THETA ** (-np.arange(half, dtype=np.float32) / np.float32(half))


def _mla_column_maps():
    used = MLA_NOPE + MLA_ROPE
    q_cols = np.zeros((MLA_HEADS, HEAD_PAD), np.int32)
    k_cols = np.zeros((MLA_HEADS, HEAD_PAD), np.int32)
    valid_q = np.zeros((MLA_HEADS, HEAD_PAD), bool)
    valid_k = np.zeros((MLA_HEADS, HEAD_PAD), bool)
    for h in range(MLA_HEADS):
        q_cols[h, :used] = h * used + np.arange(used)
        valid_q[h, :used] = True
        k_cols[h, :MLA_NOPE] = h * (MLA_NOPE + MLA_V) + np.arange(MLA_NOPE)
        valid_k[h, :MLA_NOPE] = True
    v_cols = np.zeros((MLA_HEADS, V_ROWS), np.int32)
    valid_v = np.zeros((MLA_HEADS, V_ROWS), bool)
    for h in range(MLA_HEADS):
        v_cols[h, :MLA_V] = h * (MLA_NOPE + MLA_V) + MLA_NOPE + np.arange(MLA_V)
        valid_v[h, :MLA_V] = True
    return (q_cols.reshape(-1), valid_q.reshape(-1), k_cols.reshape(-1), valid_k.reshape(-1),
            v_cols.reshape(-1), valid_v.reshape(-1))


def kernel(x, mem, positions, g_mix, w_in, b_gate, g_q_lat, w_uq, g_kv_lat, w_ukv, w_proj_mla, g_ret,
           w_proj_ret, w_out, g_cross, g_mem, w_xq, w_xkv, w_xo, g_ffn, w_up, w_conv, b_conv, w_down, g_final):
    B, S, D = x.shape
    T = B * S
    depth = w_in.shape[0]
    d_ff = w_down.shape[1]
    hw = RET_HEADS * RET_DK
    tm = min(512, S)
    tq = min(512, S)
    tr = min(512, S)
    tf = min(512, S)

    half_a = MLA_ROPE // 2
    inv_pack = np.zeros((1, LANES), np.float32)
    inv_pack[0, :RET_DK // 2] = _rope_inv(RET_DK // 2)
    inv_pack[0, MLA_NOPE:MLA_NOPE + half_a] = _rope_inv(half_a)
    q_cols, valid_q, k_cols, valid_k, v_cols, valid_v = _mla_column_maps()
    v_ones = jnp.asarray(np.where(valid_v, 0.0, 1.0)[:, None], F32)

    h = x.reshape(T, D)
    pos2 = positions.reshape(T, 1)
    row2 = lambda a: a.reshape(1, -1)
    for l in range(depth):
        w_in_p = _w_in_prep(w_in[l].T)
        w_uq_p = jnp.where(valid_q[None, :], w_uq[l][:, q_cols], 0.0).astype(BF16)
        w_uk_p = jnp.where(valid_k[None, :], w_ukv[l][:, k_cols], 0.0).astype(BF16)
        w_uv = jnp.where(valid_v[None, :], w_ukv[l][:, v_cols], 0.0).T.astype(BF16)
        assert d_ff % FF_CHUNK == 0
        w_d = w_down[l].astype(BF16)

        mkv = _mem_kv(mem, row2(g_mem[l]), w_xkv[l].astype(BF16))
        q, k, vt, rq, rk, rv, rg, gates = _in_proj(
            h, pos2, row2(g_mix[l]), w_in_p, row2(b_gate[l]), row2(g_q_lat[l]), w_uq_p,
            row2(g_kv_lat[l]), w_uk_p, w_uv, v_ones, jnp.asarray(inv_pack), tm, tq)
        seq3 = lambda a: a.reshape(B, S, a.shape[-1])
        o_a = _mla_attn(seq3(q), seq3(k), vt.reshape(B, S // tq, vt.shape[1], tq), tq)
        y_rg = _retention(seq3(rq), seq3(rk), seq3(rv), seq3(rg), row2(g_ret[l]), tr)
        h = _mix_cross(h, o_a.reshape(T, -1), y_rg.reshape(T, -1), gates, w_proj_mla[l].astype(BF16),
                       w_proj_ret[l].astype(BF16), w_out[l].astype(BF16), row2(g_cross[l]),
                       w_xq[l].astype(BF16), mkv, w_xo[l].astype(BF16), tm, S)
        last = l == depth - 1
        assert last, "only a single layer stack is supported"
        h = _ffn_out(h, row2(g_ffn[l]), w_up[l].astype(BF16), w_conv[l], row2(b_conv[l]), w_d, row2(g_final), tf, S)
    return h.reshape(B, S, D)
```

```python
import functools

import numpy as np
import jax
import jax.numpy as jnp
from jax import lax
from jax.experimental import pallas as pl
from jax.experimental.pallas import tpu as pltpu

MLA_HEADS = 8
MLA_NOPE = 64
MLA_ROPE = 32
MLA_V = 64
MLA_Q_RANK = 256
MLA_KV_RANK = 128
RET_HEADS = 4
RET_DK = 128
RET_DV = 128
RET_CHUNK = 128
X_HEADS = 4
CONV_W = 3
ROPE_THETA = 10000.0
EPS = 1e-6

LANES = 128
SUBLANES = 8
HEAD_PAD = 128
V_ROWS = MLA_V + 16
LOOP_BLOCKS = 8
FF_CHUNK = 256
VMEM_LIMIT = 56 * 1024 * 1024

BF16 = jnp.bfloat16
F32 = jnp.float32


def _rms(x, g):
    return x * lax.rsqrt(jnp.mean(x * x, axis=-1, keepdims=True) + EPS) * g


def _dot(a, b):
    return jnp.dot(a, b, preferred_element_type=F32)


def _dot_nt(a, b):
    return lax.dot_general(a, b, (((1,), (1,)), ((), ())), preferred_element_type=F32)


def _dot_tn(a, b):
    return lax.dot_general(a, b, (((0,), (0,)), ((), ())), preferred_element_type=F32)


def _const_spec(shape):
    nd = len(shape)
    return pl.BlockSpec(shape, lambda *_: (0,) * nd, pipeline_mode=pl.Buffered(1))


def _params(sem):
    return pltpu.CompilerParams(dimension_semantics=sem, vmem_limit_bytes=VMEM_LIMIT)


def _mem_kv_kernel(mem_ref, g_ref, w_ref, o_ref):
    n = _rms(mem_ref[...], g_ref[...]).astype(BF16)
    o_ref[...] = _dot(n, w_ref[...]).astype(o_ref.dtype)


def _mem_kv(mem, g_mem, w_xkv):
    B, M, D = mem.shape
    N = w_xkv.shape[1]
    return pl.pallas_call(
        _mem_kv_kernel,
        grid=(B,),
        in_specs=[pl.BlockSpec((None, M, D), lambda b: (b, 0, 0)),
                  _const_spec((1, D)), _const_spec((D, N))],
        out_specs=pl.BlockSpec((None, M, N), lambda b: (b, 0, 0)),
        out_shape=jax.ShapeDtypeStruct((B, M, N), BF16),
        compiler_params=_params(("parallel",)),
        name="mem_kv",
    )(mem, g_mem, w_xkv)


def _w_in_prep_kernel(wt_ref, o_ref):
    c0 = MLA_Q_RANK + MLA_KV_RANK
    c1 = c0 + MLA_ROPE
    cols = wt_ref.shape[1]
    o_ref[:c0, :] = wt_ref[:c0, :].astype(o_ref.dtype)
    o_ref[c0:c0 + MLA_NOPE, :] = jnp.zeros((MLA_NOPE, cols), o_ref.dtype)
    o_ref[c0 + MLA_NOPE:c0 + MLA_NOPE + MLA_ROPE, :] = wt_ref[c0:c1, :].astype(o_ref.dtype)
    o_ref[c0 + MLA_NOPE + MLA_ROPE:c0 + HEAD_PAD, :] = jnp.zeros((HEAD_PAD - MLA_NOPE - MLA_ROPE, cols), o_ref.dtype)
    o_ref[c0 + HEAD_PAD:, :] = wt_ref[c1:, :].astype(o_ref.dtype)


def _w_in_prep(w_in_t, tc=256):
    N, D = w_in_t.shape
    n_out = N - MLA_ROPE + HEAD_PAD
    return pl.pallas_call(
        _w_in_prep_kernel,
        grid=(D // tc,),
        in_specs=[pl.BlockSpec((N, tc), lambda i: (0, i))],
        out_specs=pl.BlockSpec((n_out, tc), lambda i: (0, i)),
        out_shape=jax.ShapeDtypeStruct((n_out, D), BF16),
        compiler_params=_params(("parallel",)),
        name="w_in_prep",
    )(w_in_t)


def _in_proj_kernel(x_ref, pos_ref, gmix_ref, win_ref, bgate_ref,
                    gq_ref, wuq_ref, gkv_ref, wuk_ref, wuv_ref, vones_ref, inv_ref,
                    q_ref, k_ref, vt_ref, rq_ref, rk_ref, rv_ref, rg_ref, gate_ref):
    lat_w = MLA_Q_RANK + MLA_KV_RANK + HEAD_PAD
    hw = RET_HEADS * RET_DK
    u = _rms(x_ref[...], gmix_ref[...]).astype(BF16)
    lat = _dot_nt(u, win_ref[:lat_w, :])
    ret = _dot_nt(u, win_ref[lat_w:lat_w + 4 * hw, :])
    gate_pre = _dot_nt(u, win_ref[lat_w + 4 * hw:, :])

    ang = pos_ref[...].astype(F32) * inv_ref[...]
    cos_p, sin_p = jnp.cos(ang), jnp.sin(ang)
    lane = lax.broadcasted_iota(jnp.int32, ang.shape, 1)
    half_r = RET_DK // 2
    half_a = MLA_ROPE // 2
    cos_r = jnp.where(lane < half_r, cos_p, pltpu.roll(cos_p, half_r, 1))
    sin_r = jnp.where(lane < half_r, -sin_p, pltpu.roll(sin_p, half_r, 1))
    x1_a = jnp.logical_and(lane >= MLA_NOPE, lane < MLA_NOPE + half_a)
    x2_a = jnp.logical_and(lane >= MLA_NOPE + half_a, lane < MLA_NOPE + MLA_ROPE)
    cos_a = jnp.where(x1_a, cos_p, jnp.where(x2_a, pltpu.roll(cos_p, half_a, 1), 1.0))
    sin_a = jnp.where(x1_a, -sin_p, jnp.where(x2_a, pltpu.roll(sin_p, half_a, 1), 0.0))

    def rot_a(blk):
        partner = jnp.where(x1_a,
                            pltpu.roll(blk, HEAD_PAD - half_a, 1),
                            pltpu.roll(blk, half_a, 1))
        return blk * cos_a + partner * sin_a

    cq = _rms(lat[:, :MLA_Q_RANK], gq_ref[...]).astype(BF16)
    ckv = _rms(lat[:, MLA_Q_RANK:MLA_Q_RANK + MLA_KV_RANK], gkv_ref[...]).astype(BF16)
    kr = rot_a(lat[:, MLA_Q_RANK + MLA_KV_RANK:])

    qf = _dot(cq, wuq_ref[...])
    kf = _dot(ckv, wuk_ref[...])
    q_scale = (MLA_NOPE + MLA_ROPE) ** -0.5 * np.log2(np.e)
    for h in range(MLA_HEADS):
        sl = slice(h * HEAD_PAD, (h + 1) * HEAD_PAD)
        q_ref[:, sl] = (rot_a(qf[:, sl]) * q_scale).astype(q_ref.dtype)
        k_ref[:, sl] = (kf[:, sl] + kr).astype(k_ref.dtype)
    vt_ref[...] = (_dot_nt(wuv_ref[...], ckv) + vones_ref[...]).astype(vt_ref.dtype)

    k_scale = RET_DK ** -0.5
    for h in range(RET_HEADS):
        sl = slice(h * RET_DK, (h + 1) * RET_DK)
        bq = ret[:, sl]
        rq_ref[:, sl] = (bq * cos_r + pltpu.roll(bq, RET_DK // 2, 1) * sin_r).astype(rq_ref.dtype)
        bk = ret[:, hw + h * RET_DK: hw + (h + 1) * RET_DK]
        rk_ref[:, sl] = ((bk * cos_r + pltpu.roll(bk, RET_DK // 2, 1) * sin_r) * k_scale).astype(rk_ref.dtype)
    rv_ref[...] = ret[:, 2 * hw:3 * hw].astype(rv_ref.dtype)
    rg = ret[:, 3 * hw:]
    rg_ref[...] = (rg * jax.nn.sigmoid(rg)).astype(rg_ref.dtype)

    gate_ref[...] = jax.nn.sigmoid(gate_pre + bgate_ref[...]).astype(gate_ref.dtype)


def _in_proj(x2, pos2, g_mix, w_in_p, b_gate, g_q, w_uq, g_kv, w_uk, w_uv, v_ones,
             inv_pack, tm, tkv):
    T, D = x2.shape
    hw = RET_HEADS * RET_DK
    row = lambda n: pl.BlockSpec((tm, n), lambda i: (i, 0))
    consts = [g_mix, w_in_p, b_gate, g_q, w_uq, g_kv, w_uk, w_uv, v_ones, inv_pack]
    out_widths = [MLA_HEADS * HEAD_PAD, MLA_HEADS * HEAD_PAD, hw, hw, hw, hw, 2 * D]
    out_dtypes = [BF16, BF16, F32, F32, BF16, BF16, BF16]
    vw = MLA_HEADS * V_ROWS
    per = tkv // tm
    vt_spec = pl.BlockSpec((None, vw, tm), lambda i: (i // per, 0, i % per))
    vt_shape = jax.ShapeDtypeStruct((T // tkv, vw, tkv), BF16)
    out_specs = [row(n) for n in out_widths]
    out_shape = [jax.ShapeDtypeStruct((T, n), dt) for n, dt in zip(out_widths, out_dtypes)]
    return pl.pallas_call(
        _in_proj_kernel,
        grid=(T // tm,),
        in_specs=[row(D), row(1)] + [_const_spec(c.shape) for c in consts],
        out_specs=out_specs[:2] + [vt_spec] + out_specs[2:],
        out_shape=out_shape[:2] + [vt_shape] + out_shape[2:],
        compiler_params=_params(("parallel",)),
        name="in_proj",
    )(x2, pos2, *consts)


def _mla_attn_kernel(q_ref, k_ref, vt_ref, o_ref, s_ref, mblk_ref, m_ref, acc_ref, *, tq):
    i = pl.program_id(2)
    heads = q_ref.shape[1] // HEAD_PAD

    m_ref[...] = jnp.full(m_ref.shape, -jnp.inf, F32)
    acc_ref[...] = jnp.zeros(acc_ref.shape, F32)

    def scores(j, slot):
        k0 = pl.multiple_of(j * tq, tq)
        for h in range(heads):
            sl = slice(h * HEAD_PAD, (h + 1) * HEAD_PAD)
            s = _dot_nt(k_ref[pl.ds(k0, tq), sl], q_ref[:, sl])
            s_ref[slot, h] = s
            mblk_ref[slot, h] = jnp.max(s, axis=0, keepdims=True)

    def consume(j, slot, diagonal):
        if diagonal:
            return consume_diagonal(j, slot)
        for h in range(heads):
            s = s_ref[slot, h]
            m_old = m_ref[h]
            m_new = jnp.maximum(m_old, mblk_ref[slot, h])
            alpha = jnp.exp2(m_old - m_new)
            p = jnp.exp2(s - m_new).astype(BF16)
            vt = vt_ref[j, h * V_ROWS:(h + 1) * V_ROWS, :]
            acc_ref[h] = alpha * acc_ref[h] + _dot(vt, p)
            m_ref[h] = m_new

    def consume_diagonal(j, slot):
        hq = tq // 2
        tri = (lax.broadcasted_iota(jnp.int32, (hq, hq), 0) <= lax.broadcasted_iota(jnp.int32, (hq, hq), 1))
        for h in range(heads):
            top_l = jnp.where(tri, s_ref[slot, h, :hq, :hq], -jnp.inf)
            top_r = s_ref[slot, h, :hq, hq:]
            bot_r = jnp.where(tri, s_ref[slot, h, hq:, hq:], -jnp.inf)
            m_blk = jnp.concatenate(
                [jnp.max(top_l, axis=0, keepdims=True),
                 jnp.maximum(jnp.max(top_r, axis=0, keepdims=True), jnp.max(bot_r, axis=0, keepdims=True))], axis=1)
            m_old = m_ref[h]
            m_new = jnp.maximum(m_old, m_blk)
            alpha = jnp.exp2(m_old - m_new)
            p_top = jnp.exp2(jnp.concatenate([top_l, top_r], axis=1) - m_new).astype(BF16)
            p_bot = jnp.exp2(bot_r - m_new[:, hq:]).astype(BF16)
            vt = vt_ref[j, h * V_ROWS:(h + 1) * V_ROWS, :]
            pv = _dot(vt[:, :hq], p_top)
            pv = jnp.concatenate([pv[:, :hq], pv[:, hq:] + _dot(vt[:, hq:], p_bot)], axis=1)
            acc_ref[h] = alpha * acc_ref[h] + pv
            m_ref[h] = m_new

    scores(0, 0)

    def run(first, count, with_diagonal):
        for d in range(count):
            scores(first + d + 1, (d + 1) % 2)
            consume(first + d, d % 2, False)
        if with_diagonal:
            consume(first + count, count % 2, True)

    def body(t, carry):
        run(LOOP_BLOCKS * t, LOOP_BLOCKS, False)
        return carry

    lax.fori_loop(0, i // LOOP_BLOCKS, body, 0)
    for r in range(LOOP_BLOCKS):
        @pl.when(i % LOOP_BLOCKS == r)
        def _(r=r):
            run(i - r, r, True)

    out_t = jnp.concatenate([acc_ref[h, :MLA_V] / acc_ref[h, MLA_V:MLA_V + 1] for h in range(heads)], axis=0)
    o_ref[...] = out_t.T.astype(o_ref.dtype)


def _mla_attn(q, k, vt, tq, heads_per_step=2):
    B, S, _ = q.shape
    groups = MLA_HEADS // heads_per_step
    qw = heads_per_step * HEAD_PAD
    return pl.pallas_call(
        functools.partial(_mla_attn_kernel, tq=tq),
        grid=(B, groups, S // tq),
        in_specs=[pl.BlockSpec((None, tq, qw), lambda b, g, i: (b, i, g)),
                  pl.BlockSpec((None, S, qw), lambda b, g, i: (b, 0, g)),
                  pl.BlockSpec((None, S // tq, heads_per_step * V_ROWS, tq), lambda b, g, i: (b, 0, g, 0))],
        out_specs=pl.BlockSpec((None, tq, heads_per_step * MLA_V), lambda b, g, i: (b, i, g)),
        out_shape=jax.ShapeDtypeStruct((B, S, MLA_HEADS * MLA_V), BF16),
        scratch_shapes=[pltpu.VMEM((2, heads_per_step, tq, tq), F32),
                        pltpu.VMEM((2, heads_per_step, 1, tq), F32),
                        pltpu.VMEM((heads_per_step, 1, tq), F32),
                        pltpu.VMEM((heads_per_step, V_ROWS, tq), F32)],
        compiler_params=_params(("parallel", "parallel", "parallel")),
        name="mla_attn",
    )(q, k, vt)


def _retention_tables():
    C = RET_CHUNK
    log_g = np.log1p(-np.exp2(-5.0 - np.arange(RET_HEADS, dtype=np.float64)))
    idx = np.arange(C, dtype=np.float64)
    rel = idx[:, None] - idx[None, :]
    dmask = np.where(rel >= 0, np.exp(log_g[:, None, None] * np.maximum(rel, 0.0)), 0.0)
    zeta = np.exp(log_g[:, None] * (C - 1.0 - idx)[None, :])
    xi = np.exp(log_g[:, None] * (idx + 1.0)[None, :])
    decay = np.exp(log_g * C)
    f = lambda a: jnp.asarray(a, F32)
    return f(dmask), f(zeta[:, :, None]), f(xi[:, :, None]), f(np.broadcast_to(decay[:, None, None], (RET_HEADS, 1, LANES)))


def _retention_kernel(q_ref, k_ref, v_ref, rg_ref, gret_ref, dmask_ref, zeta_ref, xi_ref, decay_ref,
                      o_ref, state_ref, *, chunks):
    @pl.when(pl.program_id(1) == 0)
    def _():
        state_ref[...] = jnp.zeros(state_ref.shape, F32)

    C = RET_CHUNK
    for h in range(RET_HEADS):
        sl = slice(h * RET_DK, (h + 1) * RET_DK)
        dmask, zeta, xi = dmask_ref[h], zeta_ref[h], xi_ref[h]
        decay = decay_ref[h][:, :RET_DV]
        inner, kv, qx = [], [], []
        for c in range(chunks):
            rows = slice(c * C, (c + 1) * C)
            q, k, v = q_ref[rows, sl], k_ref[rows, sl], v_ref[rows, sl]
            s = (_dot_nt(q.astype(BF16), k.astype(BF16)) * dmask).astype(BF16)
            inner.append(_dot(s, v))
            kv.append(_dot_tn((k * zeta).astype(BF16), v))
            qx.append((q * xi).astype(BF16))
        state = state_ref[h]
        for c in range(chunks):
            rows = slice(c * C, (c + 1) * C)
            y = inner[c] + _dot(qx[c], state.astype(BF16))
            state = state * decay + kv[c]
            mu = jnp.mean(y, axis=-1, keepdims=True)
            yc = y - mu
            var = jnp.mean(yc * yc, axis=-1, keepdims=True)
            yn = yc * lax.rsqrt(var + EPS) * gret_ref[:, sl]
            o_ref[rows, sl] = (rg_ref[rows, sl].astype(F32) * yn).astype(o_ref.dtype)
        state_ref[h] = state


def _retention(rq, rk, rv, rg, g_ret, tr):
    B, S, W = rq.shape
    tables = _retention_tables()
    blk = pl.BlockSpec((None, tr, W), lambda b, i: (b, i, 0))
    return pl.pallas_call(
        functools.partial(_retention_kernel, chunks=tr // RET_CHUNK),
        grid=(B, S // tr),
        in_specs=[blk, blk, blk, blk, _const_spec(g_ret.shape)] + [_const_spec(t.shape) for t in tables],
        out_specs=blk,
        out_shape=jax.ShapeDtypeStruct((B, S, W), BF16),
        scratch_shapes=[pltpu.VMEM((RET_HEADS, RET_DK, RET_DV), F32)],
        compiler_params=_params(("parallel", "arbitrary")),
        name="retention",
    )(rq, rk, rv, rg, g_ret, *tables)


def _mix_cross_kernel(x_ref, oa_ref, yr_ref, gate_ref, wpa_ref, wpr_ref, wout_ref, gc_ref, wxq_ref,
                      mkv_ref, wxo_ref, h_ref):
    D = x_ref.shape[1]
    hd = D // X_HEADS
    y_a = _dot(oa_ref[...], wpa_ref[...])
    y_r = _dot(yr_ref[...], wpr_ref[...])
    merged = gate_ref[:, :D].astype(F32) * y_a + gate_ref[:, D:].astype(F32) * y_r
    h1 = x_ref[...] + _dot(merged.astype(BF16), wout_ref[...])

    xq = _dot(_rms(h1, gc_ref[...]).astype(BF16), wxq_ref[...]).astype(BF16)
    scale = hd ** -0.5
    outs = []
    for h in range(X_HEADS):
        sl = slice(h * hd, (h + 1) * hd)
        s = _dot_nt(xq[:, sl], mkv_ref[:, sl]) * scale
        e = jnp.exp(s - jnp.max(s, axis=-1, keepdims=True))
        o = _dot(e.astype(BF16), mkv_ref[:, D + h * hd: D + (h + 1) * hd])
        outs.append((o / jnp.sum(e, axis=-1, keepdims=True)).astype(BF16))
    xo = jnp.concatenate(outs, axis=-1)
    h_ref[...] = h1 + _dot(xo, wxo_ref[...])


def _mix_cross(x2, o_a, y_rg, gates, w_pa, w_pr, w_out, g_cross, w_xq, mkv, w_xo, tm, seq):
    T, D = x2.shape
    M = mkv.shape[1]
    row = lambda n: pl.BlockSpec((tm, n), lambda i: (i, 0))
    steps_per_batch = seq // tm
    return pl.pallas_call(
        _mix_cross_kernel,
        grid=(T // tm,),
        in_specs=[row(D), row(o_a.shape[1]), row(y_rg.shape[1]), row(gates.shape[1]),
                  _const_spec(w_pa.shape), _const_spec(w_pr.shape), _const_spec(w_out.shape),
                  _const_spec(g_cross.shape), _const_spec(w_xq.shape),
                  pl.BlockSpec((None, M, 2 * D), lambda i: (i // steps_per_batch, 0, 0)),
                  _const_spec(w_xo.shape)],
        out_specs=row(D),
        out_shape=jax.ShapeDtypeStruct((T, D), F32),
        compiler_params=_params(("parallel",)),
        name="mix_cross",
    )(x2, o_a, y_rg, gates, w_pa, w_pr, w_out, g_cross, w_xq, mkv, w_xo)


def _ffn_kernel(h_ref, gf_ref, wu_ref, wc_ref, bc_ref, wd_ref, gfin_ref,
                o_ref, n_ref, act_ref, haloa_ref, halob_ref, *, steps_per_batch):
    tm = h_ref.shape[0]
    nch, _, fc = haloa_ref.shape
    d_ff = nch * fc
    halo = SUBLANES

    @pl.when(pl.program_id(0) % steps_per_batch == 0)
    def _():
        haloa_ref[...] = jnp.zeros(haloa_ref.shape, F32)
        halob_ref[...] = jnp.zeros(halob_ref.shape, F32)

    n_ref[...] = _rms(h_ref[...], gf_ref[...]).astype(BF16)

    sub = lax.broadcasted_iota(jnp.int32, (SUBLANES, fc), 0)

    def conv(halo_ref, col, up, j):
        prev = halo_ref[j]
        halo_ref[j] = up[tm - halo:, :]
        w = wc_ref[:, col:col + fc]
        out = up * w[CONV_W - 1:CONV_W, :] + bc_ref[:, col:col + fc]
        for t in range(1, CONV_W):
            rolled = pltpu.roll(up, t, 0)
            head = jnp.where(sub < t, pltpu.roll(prev, t, 0), rolled[:halo, :])
            shifted = jnp.concatenate([head, rolled[halo:, :]], axis=0)
            out = out + shifted * w[CONV_W - 1 - t:CONV_W - t, :]
        return out

    n = n_ref[...]
    up_pair = lambda j: (_dot(n, wu_ref[:, j * fc:(j + 1) * fc]), _dot(n, wu_ref[:, d_ff + j * fc:d_ff + (j + 1) * fc]))
    ups = up_pair(0)
    for j in range(nch):
        ua, ub = ups
        if j + 1 < nch:
            ups = up_pair(j + 1)
        a = conv(haloa_ref, j * fc, ua, j)
        b = conv(halob_ref, d_ff + j * fc, ub, j)
        act_ref[:, j * fc:(j + 1) * fc] = (a * jax.nn.sigmoid(a) * b).astype(BF16)

    o_ref[...] = _rms(h_ref[...] + _dot(act_ref[...], wd_ref[...]), gfin_ref[...])


def _ffn_out(h2, g_ffn, w_u, w_c, b_c, w_d, g_final, tm, seq):
    T, D = h2.shape
    fc = FF_CHUNK
    nch = w_d.shape[0] // fc
    row = pl.BlockSpec((tm, D), lambda i: (i, 0))
    consts = [g_ffn, w_u, w_c, b_c, w_d, g_final]
    return pl.pallas_call(
        functools.partial(_ffn_kernel, steps_per_batch=seq // tm),
        grid=(T // tm,),
        in_specs=[row] + [_const_spec(c.shape) for c in consts],
        out_specs=row,
        out_shape=jax.ShapeDtypeStruct((T, D), F32),
        scratch_shapes=[pltpu.VMEM((tm, D), BF16), pltpu.VMEM((tm, nch * fc), BF16),
                        pltpu.VMEM((nch, SUBLANES, fc), F32), pltpu.VMEM((nch, SUBLANES, fc), F32)],
        compiler_params=_params(("arbitrary",)),
        name="ffn_out",
    )(h2, *consts)


def _rope_inv(half):
    return ROPE_THETA ** (-np.arange(half, dtype=np.float32) / np.float32(half))


def _mla_column_maps():
    used = MLA_NOPE + MLA_ROPE
    q_cols = np.zeros((MLA_HEADS, HEAD_PAD), np.int32)
    k_cols = np.zeros((MLA_HEADS, HEAD_PAD), np.int32)
    valid_q = np.zeros((MLA_HEADS, HEAD_PAD), bool)
    valid_k = np.zeros((MLA_HEADS, HEAD_PAD), bool)
    for h in range(MLA_HEADS):
        q_cols[h, :used] = h * used + np.arange(used)
        valid_q[h, :used] = True
        k_cols[h, :MLA_NOPE] = h * (MLA_NOPE + MLA_V) + np.arange(MLA_NOPE)
        valid_k[h, :MLA_NOPE] = True
    v_cols = np.zeros((MLA_HEADS, V_ROWS), np.int32)
    valid_v = np.zeros((MLA_HEADS, V_ROWS), bool)
    for h in range(MLA_HEADS):
        v_cols[h, :MLA_V] = h * (MLA_NOPE + MLA_V) + MLA_NOPE + np.arange(MLA_V)
        valid_v[h, :MLA_V] = True
    return (q_cols.reshape(-1), valid_q.reshape(-1), k_cols.reshape(-1), valid_k.reshape(-1),
            v_cols.reshape(-1), valid_v.reshape(-1))


def kernel(x, mem, positions, g_mix, w_in, b_gate, g_q_lat, w_uq, g_kv_lat, w_ukv, w_proj_mla, g_ret,
           w_proj_ret, w_out, g_cross, g_mem, w_xq, w_xkv, w_xo, g_ffn, w_up, w_conv, b_conv, w_down, g_final):
    B, S, D = x.shape
    T = B * S
    depth = w_in.shape[0]
    d_ff = w_down.shape[1]
    hw = RET_HEADS * RET_DK
    tm = min(512, S)
    tq = min(512, S)
    tr = min(512, S)
    tf = min(512, S)

    half_a = MLA_ROPE // 2
    inv_pack = np.zeros((1, LANES), np.float32)
    inv_pack[0, :RET_DK // 2] = _rope_inv(RET_DK // 2)
    inv_pack[0, MLA_NOPE:MLA_NOPE + half_a] = _rope_inv(half_a)
    q_cols, valid_q, k_cols, valid_k, v_cols, valid_v = _mla_column_maps()
    v_ones = jnp.asarray(np.where(valid_v, 0.0, 1.0)[:, None], F32)

    h = x.reshape(T, D)
    pos2 = positions.reshape(T, 1)
    row2 = lambda a: a.reshape(1, -1)
    for l in range(depth):
        w_in_p = _w_in_prep(w_in[l].T)
        w_uq_p = jnp.where(valid_q[None, :], w_uq[l][:, q_cols], 0.0).astype(BF16)
        w_uk_p = jnp.where(valid_k[None, :], w_ukv[l][:, k_cols], 0.0).astype(BF16)
        w_uv = jnp.where(valid_v[None, :], w_ukv[l][:, v_cols], 0.0).T.astype(BF16)
        assert d_ff % FF_CHUNK == 0
        w_d = w_down[l].astype(BF16)

        mkv = _mem_kv(mem, row2(g_mem[l]), w_xkv[l].astype(BF16))
        q, k, vt, rq, rk, rv, rg, gates = _in_proj(
            h, pos2, row2(g_mix[l]), w_in_p, row2(b_gate[l]), row2(g_q_lat[l]), w_uq_p,
            row2(g_kv_lat[l]), w_uk_p, w_uv, v_ones, jnp.asarray(inv_pack), tm, tq)
        seq3 = lambda a: a.reshape(B, S, a.shape[-1])
        o_a = _mla_attn(seq3(q), seq3(k), vt.reshape(B, S // tq, vt.shape[1], tq), tq)
        y_rg = _retention(seq3(rq), seq3(rk), seq3(rv), seq3(rg), row2(g_ret[l]), tr)
        h = _mix_cross(h, o_a.reshape(T, -1), y_rg.reshape(T, -1), gates, w_proj_mla[l].astype(BF16),
                       w_proj_ret[l].astype(BF16), w_out[l].astype(BF16), row2(g_cross[l]),
                       w_xq[l].astype(BF16), mkv, w_xo[l].astype(BF16), tm, S)
        last = l == depth - 1
        assert last, "only a single layer stack is supported"
        h = _ffn_out(h, row2(g_ffn[l]), w_up[l].astype(BF16), w_conv[l], row2(b_conv[l]), w_d, row2(g_final), tf, S)
    return h.reshape(B, S, D)
```
